```python
import jax, jax.numpy as jnp
from jax import lax
import numpy as np

D_MODEL = 1024
BATCH = 32
SEQ = 2048
DEPTH = 1

MEM_LEN = 256

HEAD_DIM = 64
NSA_HEADS = D_MODEL // 128
NSA_KV_HEADS = NSA_HEADS // 4
NSA_GROUP = NSA_HEADS // NSA_KV_HEADS
NSA_WIDTH = NSA_HEADS * HEAD_DIM
NSA_KV_WIDTH = NSA_KV_HEADS * HEAD_DIM
CMP_BLOCK = 32
CMP_STRIDE = 16
CMP_HIDDEN = 2 * HEAD_DIM
SLC_BLOCK = 64
N_SELECT = 8
WINDOW = 256
NSA_QBLOCK = SLC_BLOCK

GLA_HEADS = 4
GLA_DK = 64
GLA_DV = 128
GLA_KEY_WIDTH = GLA_HEADS * GLA_DK
GLA_VAL_WIDTH = GLA_HEADS * GLA_DV
GLA_GATE_RANK = 16
GLA_TAU = 16.0
GLA_CHUNK = 64

XATTN_HEADS = 4
XATTN_HEAD_DIM = 128
XATTN_WIDTH = XATTN_HEADS * XATTN_HEAD_DIM

FFN_DIM = 2816
CONV_WIDTH = 3

RMS_EPS = 1e-6
NEG_INF = -1e30

IN_SPLITS = (NSA_WIDTH, 6 * NSA_KV_WIDTH, 3 * NSA_HEADS,
             GLA_KEY_WIDTH, GLA_KEY_WIDTH, GLA_VAL_WIDTH, GLA_VAL_WIDTH, GLA_GATE_RANK,
             2 * D_MODEL)
IN_WIDTH = (NSA_WIDTH + 6 * NSA_KV_WIDTH + 3 * NSA_HEADS + 2 * GLA_KEY_WIDTH
            + 2 * GLA_VAL_WIDTH + GLA_GATE_RANK + 2 * D_MODEL)

kernel_name = "hybrid_nsa_gla_gated_convffn"


def rms_norm(x, g):
    x32 = x.astype(jnp.float32)
    y = x32 * lax.rsqrt(jnp.mean(x32 * x32, axis=-1, keepdims=True) + RMS_EPS)
    return (y * g.astype(jnp.float32)).astype(x.dtype)


def alibi_slopes(n):
    return 2.0 ** (-8.0 * jnp.arange(1, n + 1, dtype=jnp.float32) / n)


def masked_softmax(s, mask):
    s = jnp.where(mask, s, NEG_INF)
    return jnp.where(mask, jax.nn.softmax(s, axis=-1), 0.0)


def compress_blocks(kv, pos_emb, w1, w2):
    B, S, H, dh = kv.shape
    c = kv.reshape(B, S // CMP_STRIDE, CMP_STRIDE, H, dh)
    blocks = jnp.concatenate([c[:, :-1], c[:, 1:]], axis=2) + pos_emb[None, None, :, None, :]
    n_cmp = blocks.shape[1]
    flat = blocks.transpose(0, 1, 3, 2, 4).reshape(B, n_cmp, H, CMP_BLOCK * dh)
    return jax.nn.gelu(flat @ w1) @ w2


def nsa_attention(q, kc, vc, ks, vs, kw, vw, gate_logits,
                  cmp_pos_k, cmp_w1_k, cmp_w2_k, cmp_pos_v, cmp_w1_v, cmp_w2_v):
    B, S = q.shape[:2]
    f32 = jnp.float32
    q = q.reshape(B, S, NSA_KV_HEADS, NSA_GROUP, HEAD_DIM) * (HEAD_DIM ** -0.5)
    kv_shape = (B, S, NSA_KV_HEADS, HEAD_DIM)
    kc, vc, ks, vs, kw, vw = [a.reshape(kv_shape) for a in (kc, vc, ks, vs, kw, vw)]
    gates = jax.nn.sigmoid(gate_logits.astype(f32)).reshape(B, S, NSA_KV_HEADS, NSA_GROUP, 3)
    slopes = alibi_slopes(NSA_HEADS).reshape(NSA_KV_HEADS, NSA_GROUP)

    k_cmp = compress_blocks(kc, cmp_pos_k, cmp_w1_k, cmp_w2_k)
    v_cmp = compress_blocks(vc, cmp_pos_v, cmp_w1_v, cmp_w2_v)
    n_cmp = k_cmp.shape[1]
    cmp_end = jnp.arange(n_cmp) * CMP_STRIDE + CMP_BLOCK - 1

    n_slc = S // SLC_BLOCK
    n_sel = min(N_SELECT, n_slc)
    cmp_tok = jnp.arange(n_cmp)[:, None] * CMP_STRIDE + jnp.arange(CMP_BLOCK)[None, :]
    overlap = jax.nn.one_hot(cmp_tok // SLC_BLOCK, n_slc, dtype=f32).sum(axis=1) / CMP_BLOCK

    ks_blk = ks.reshape(B, n_slc, SLC_BLOCK, NSA_KV_HEADS, HEAD_DIM).transpose(0, 3, 1, 2, 4)
    vs_blk = vs.reshape(B, n_slc, SLC_BLOCK, NSA_KV_HEADS, HEAD_DIM).transpose(0, 3, 1, 2, 4)
    kw_pad = jnp.pad(kw, ((0, 0), (WINDOW, 0), (0, 0), (0, 0)))
    vw_pad = jnp.pad(vw, ((0, 0), (WINDOW, 0), (0, 0), (0, 0)))
    b_ix = jnp.arange(B)[:, None, None, None]
    h_ix = jnp.arange(NSA_KV_HEADS)[None, :, None, None]
    forced_score = float(NSA_GROUP + 1)

    def query_block(i):
        t0 = i * NSA_QBLOCK
        qb = lax.dynamic_slice_in_dim(q, t0, NSA_QBLOCK, axis=1)
        gb = lax.dynamic_slice_in_dim(gates, t0, NSA_QBLOCK, axis=1)
        t = t0 + jnp.arange(NSA_QBLOCK)

        dist = (t[:, None] - cmp_end[None, :]).astype(f32)
        s = jnp.einsum('bqhgd,bnhd->bhgqn', qb, k_cmp).astype(f32)
        s = s - slopes[:, :, None, None] * dist
        p_cmp = masked_softmax(s, dist >= 0)
        o_cmp = jnp.einsum('bhgqn,bnhd->bqhgd', p_cmp.astype(v_cmp.dtype), v_cmp)

        imp = jnp.einsum('bhgqn,nj->bhqj', p_cmp, overlap)
        cur = t // SLC_BLOCK
        j = jnp.arange(n_slc)
        forced = (j[None, :] == 0) | (j[None, :] == cur[:, None]) | (j[None, :] == cur[:, None] - 1)
        future = j[None, :] > cur[:, None]
        imp = jnp.where(forced, forced_score, jnp.where(future, -1.0, imp))
        _, idx = lax.top_k(imp, n_sel)
        kg = ks_blk[b_ix, h_ix, idx]
        vg = vs_blk[b_ix, h_ix, idx]
        kpos = idx[..., None] * SLC_BLOCK + jnp.arange(SLC_BLOCK)
        dist = (t[:, None, None] - kpos).astype(f32)[:, :, None]
        s = jnp.einsum('bqhgd,bhqnld->bhgqnl', qb, kg).astype(f32)
        s = s - slopes[None, :, :, None, None, None] * dist
        n_keys = n_sel * SLC_BLOCK
        s = s.reshape(B, NSA_KV_HEADS, NSA_GROUP, NSA_QBLOCK, n_keys)
        mask = (dist >= 0).reshape(B, NSA_KV_HEADS, 1, NSA_QBLOCK, n_keys)
        p_slc = masked_softmax(s, mask)
        vg = vg.reshape(B, NSA_KV_HEADS, NSA_QBLOCK, n_keys, HEAD_DIM)
        o_slc = jnp.einsum('bhgqk,bhqkd->bqhgd', p_slc.astype(vg.dtype), vg)

        kwb = lax.dynamic_slice_in_dim(kw_pad, t0, NSA_QBLOCK + WINDOW, axis=1)
        vwb = lax.dynamic_slice_in_dim(vw_pad, t0, NSA_QBLOCK + WINDOW, axis=1)
        kpos = t0 - WINDOW + jnp.arange(NSA_QBLOCK + WINDOW)
        dist_i = t[:, None] - kpos[None, :]
        mask = (dist_i >= 0) & (dist_i < WINDOW) & (kpos[None, :] >= 0)
        s = jnp.einsum('bqhgd,bkhd->bhgqk', qb, kwb).astype(f32)
        s = s - slopes[:, :, None, None] * dist_i.astype(f32)
        p_win = masked_softmax(s, mask)
        o_win = jnp.einsum('bhgqk,bkhd->bqhgd', p_win.astype(vwb.dtype), vwb)

        return gb[..., 0:1] * o_cmp + gb[..., 1:2] * o_slc + gb[..., 2:3] * o_win

    out = lax.map(query_block, jnp.arange(S // NSA_QBLOCK))
    out = out.transpose(1, 0, 2, 3, 4, 5).reshape(B, S, NSA_WIDTH)
    return out.astype(q.dtype)


def gla(q, k, v, r, a_low, w_alpha2, b_alpha, norm_g):
    B, S = q.shape[:2]
    f32 = jnp.float32
    nc = S // GLA_CHUNK
    shp_k = (B, nc, GLA_CHUNK, GLA_HEADS, GLA_DK)
    shp_v = (B, nc, GLA_CHUNK, GLA_HEADS, GLA_DV)
    q = q.reshape(shp_k).astype(f32) * (GLA_DK ** -0.5)
    k = k.reshape(shp_k).astype(f32)
    v = v.reshape(shp_v).astype(f32)
    log_a = jax.nn.log_sigmoid((a_low @ w_alpha2 + b_alpha).astype(f32)) / GLA_TAU
    b = jnp.cumsum(log_a.reshape(shp_k), axis=2)
    b_last = b[:, :, -1:]
    q_d = q * jnp.exp(b)
    k_d = k * jnp.exp(-b)
    k_s = k * jnp.exp(b_last - b)
    causal = jnp.tril(jnp.ones((GLA_CHUNK, GLA_CHUNK), dtype=bool))
    att = jnp.where(causal, jnp.einsum('bnihd,bnjhd->bnhij', q_d, k_d), 0.0)
    o_intra = jnp.einsum('bnhij,bnjhe->bnihe', att, v)
    state_inc = jnp.einsum('bnjhd,bnjhe->bnhde', k_s, v)
    decay = jnp.exp(b_last[:, :, 0])

    def step(state, xs):
        dec, inc = xs
        return dec[..., None] * state + inc, state

    init = jnp.zeros((B, GLA_HEADS, GLA_DK, GLA_DV), f32)
    _, states = lax.scan(step, init, (decay.swapaxes(0, 1), state_inc.swapaxes(0, 1)))
    states = states.swapaxes(0, 1)
    o_inter = jnp.einsum('bnihd,bnhde->bnihe', q_d, states)
    o = (o_intra + o_inter).reshape(B, S, GLA_HEADS, GLA_DV)
    o = o * lax.rsqrt(jnp.mean(o * o, axis=-1, keepdims=True) + RMS_EPS) * norm_g.astype(f32)
    o = o * jax.nn.silu(r.reshape(B, S, GLA_HEADS, GLA_DV).astype(f32))
    return o.reshape(B, S, GLA_VAL_WIDTH).astype(r.dtype)


def memory_cross_attention(h, mem_n, w_xq, w_xkv, w_xo):
    B, S = h.shape[:2]
    M = mem_n.shape[1]
    q = (h @ w_xq).reshape(B, S, XATTN_HEADS, XATTN_HEAD_DIM) * (XATTN_HEAD_DIM ** -0.5)
    k, v = jnp.split(mem_n @ w_xkv, 2, axis=-1)
    k = k.reshape(B, M, XATTN_HEADS, XATTN_HEAD_DIM)
    v = v.reshape(B, M, XATTN_HEADS, XATTN_HEAD_DIM)
    p = jax.nn.softmax(jnp.einsum('bshd,bmhd->bhsm', q, k).astype(jnp.float32), axis=-1)
    o = jnp.einsum('bhsm,bmhd->bshd', p.astype(v.dtype), v).reshape(B, S, XATTN_WIDTH)
    return o @ w_xo


def conv_ffn(h, w_up, conv_w, conv_b, w_down):
    S = h.shape[1]
    u, g = jnp.split(h @ w_up, 2, axis=-1)
    u_pad = jnp.pad(u, ((0, 0), (CONV_WIDTH - 1, 0), (0, 0)))
    u = sum(conv_w[tap] * u_pad[:, tap:tap + S] for tap in range(CONV_WIDTH)) + conv_b
    return (jax.nn.gelu(u) * g) @ w_down


def setup_inputs(seed: int = 0) -> dict:
    key = jax.random.key(seed)
    ks = jax.random.split(key, 28)
    f32 = jnp.float32
    L = DEPTH

    def dense(k, shape, fan_in):
        return jax.random.normal(k, shape, f32) * fan_in ** -0.5

    def gain(k, shape):
        return 1.0 + 0.02 * jax.random.normal(k, shape, f32)

    def small(k, shape, scale):
        return scale * jax.random.normal(k, shape, f32)

    return {
        "x": jax.random.normal(ks[0], (BATCH, SEQ, D_MODEL), f32),
        "mem": jax.random.normal(ks[1], (BATCH, MEM_LEN, D_MODEL), f32),
        "ln_mix_g": gain(ks[2], (L, D_MODEL)),
        "w_in": dense(ks[3], (L, D_MODEL, IN_WIDTH), D_MODEL),
        "nsa_gate_b": small(ks[4], (L, 3 * NSA_HEADS), 0.1),
        "cmp_pos_k": small(ks[5], (L, CMP_BLOCK, HEAD_DIM), 0.1),
        "cmp_w1_k": dense(ks[6], (L, CMP_BLOCK * HEAD_DIM, CMP_HIDDEN), CMP_BLOCK * HEAD_DIM),
        "cmp_w2_k": dense(ks[7], (L, CMP_HIDDEN, HEAD_DIM), CMP_HIDDEN),
        "cmp_pos_v": small(ks[8], (L, CMP_BLOCK, HEAD_DIM), 0.1),
        "cmp_w1_v": dense(ks[9], (L, CMP_BLOCK * HEAD_DIM, CMP_HIDDEN), CMP_BLOCK * HEAD_DIM),
        "cmp_w2_v": dense(ks[10], (L, CMP_HIDDEN, HEAD_DIM), CMP_HIDDEN),
        "gla_w_alpha2": dense(ks[11], (L, GLA_GATE_RANK, GLA_KEY_WIDTH), GLA_GATE_RANK),
        "gla_b_alpha": small(ks[12], (L, GLA_KEY_WIDTH), 0.1),
        "gla_norm_g": gain(ks[13], (L, GLA_DV)),
        "w_branch_nsa": dense(ks[14], (L, NSA_WIDTH, D_MODEL), NSA_WIDTH),
        "w_branch_gla": dense(ks[15], (L, GLA_VAL_WIDTH, D_MODEL), GLA_VAL_WIDTH),
        "w_out": dense(ks[16], (L, D_MODEL, D_MODEL), D_MODEL),
        "ln_x_g": gain(ks[17], (L, D_MODEL)),
        "ln_mem_g": gain(ks[18], (L, D_MODEL)),
        "w_xq": dense(ks[19], (L, D_MODEL, XATTN_WIDTH), D_MODEL),
        "w_xkv": dense(ks[20], (L, D_MODEL, 2 * XATTN_WIDTH), D_MODEL),
        "w_xo": dense(ks[21], (L, XATTN_WIDTH, D_MODEL), XATTN_WIDTH),
        "ln_ffn_g": gain(ks[22], (L, D_MODEL)),
        "w_up": dense(ks[23], (L, D_MODEL, 2 * FFN_DIM), D_MODEL),
        "conv_w": dense(ks[24], (L, CONV_WIDTH, FFN_DIM), CONV_WIDTH),
        "conv_b": small(ks[25], (L, FFN_DIM), 0.02),
        "w_down": dense(ks[26], (L, FFN_DIM, D_MODEL), FFN_DIM),
        "ln_final_g": gain(ks[27], (D_MODEL,)),
    }


def reference(x, mem, ln_mix_g, w_in, nsa_gate_b, cmp_pos_k, cmp_w1_k, cmp_w2_k,
              cmp_pos_v, cmp_w1_v, cmp_w2_v, gla_w_alpha2, gla_b_alpha, gla_norm_g,
              w_branch_nsa, w_branch_gla, w_out, ln_x_g, ln_mem_g, w_xq, w_xkv, w_xo,
              ln_ffn_g, w_up, conv_w, conv_b, w_down, ln_final_g):
    split_points = np.cumsum(IN_SPLITS)[:-1].tolist()
    for l in range(DEPTH):
        h = rms_norm(x, ln_mix_g[l])
        q_a, kv_a, gate_a, q_b, k_b, v_b, r_b, alpha_b, merge = jnp.split(h @ w_in[l], split_points, axis=-1)
        kc, vc, ks_, vs_, kw, vw = jnp.split(kv_a, 6, axis=-1)
        o_nsa = nsa_attention(q_a, kc, vc, ks_, vs_, kw, vw, gate_a + nsa_gate_b[l],
                              cmp_pos_k[l], cmp_w1_k[l], cmp_w2_k[l],
                              cmp_pos_v[l], cmp_w1_v[l], cmp_w2_v[l])
        o_gla = gla(q_b, k_b, v_b, r_b, alpha_b, gla_w_alpha2[l], gla_b_alpha[l], gla_norm_g[l])
        g_nsa, g_gla = jnp.split(jax.nn.sigmoid(merge), 2, axis=-1)
        mix = g_nsa * (o_nsa @ w_branch_nsa[l]) + g_gla * (o_gla @ w_branch_gla[l])
        x = x + (mix @ w_out[l]).astype(x.dtype)
        mem_n = rms_norm(mem, ln_mem_g[l])
        x = x + memory_cross_attention(rms_norm(x, ln_x_g[l]), mem_n, w_xq[l], w_xkv[l], w_xo[l]).astype(x.dtype)
        x = x + conv_ffn(rms_norm(x, ln_ffn_g[l]), w_up[l], conv_w[l], conv_b[l], w_down[l]).astype(x.dtype)
    return rms_norm(x, ln_final_g)
```

```python
import functools

import jax
import jax.numpy as jnp
import numpy as np
from jax import lax
from jax.experimental import pallas as pl
from jax.experimental.pallas import tpu as pltpu

F32 = jnp.float32
BF16 = jnp.bfloat16
HIGHEST = lax.Precision.HIGHEST

D_MODEL = 1024
SEQ = 2048
MEM_LEN = 256
HEAD_DIM = 64
NSA_HEADS = 8
NSA_KV_HEADS = 2
NSA_GROUP = 4
CMP_BLOCK = 32
CMP_STRIDE = 16
CMP_HIDDEN = 128
N_CMP_PAD = SEQ // CMP_STRIDE
SLC_BLOCK = 64
N_SLC = SEQ // SLC_BLOCK
N_SELECT = 8
WINDOW = 256
GLA_HEADS = 4
GLA_DK = 64
GLA_DV = 128
GLA_KEY_WIDTH = 256
GLA_VAL_WIDTH = 512
GLA_RANK = 16
GLA_TAU = 16.0
GLA_CHUNK = 64
XATTN_HEADS = 4
XATTN_HEAD_DIM = 128
XATTN_WIDTH = 512
FFN_DIM = 2816
CONV_WIDTH = 3
RMS_EPS = 1e-6
NEG_INF = -1e30

LANE = 128
VMEM_LIMIT = 56 * 1024 * 1024

AUG_SEL = HEAD_DIM
AUG_HI = HEAD_DIM + N_SLC
AUG_LO = AUG_HI + 1

C_Q = 0
C_KS = C_Q + NSA_HEADS * LANE
C_KW = C_KS + NSA_KV_HEADS * LANE
C_VS = C_KW + NSA_KV_HEADS * LANE
C_VW = C_VS + NSA_KV_HEADS * LANE
C_KC = C_VW + NSA_KV_HEADS * LANE
C_VC = C_KC + LANE
C_GT = C_VC + LANE
C_QB = C_GT + LANE
C_KB = C_QB + GLA_KEY_WIDTH
C_VB = C_KB + GLA_KEY_WIDTH
C_RB = C_VB + GLA_VAL_WIDTH
C_AL = C_RB + GLA_VAL_WIDTH
C_MG = C_AL + LANE
C_END = C_MG + 2 * D_MODEL

TM = 512
WIN_PAD = 320
WIN_KEYS = WIN_PAD + SLC_BLOCK
FFN_CHUNK = 1408


def _rms(x, g):
    return x * lax.rsqrt(jnp.mean(x * x, axis=-1, keepdims=True) + RMS_EPS) * g


def _dot(a, b, **kw):
    return jnp.dot(a, b, preferred_element_type=F32, **kw)


def _dot_nt(a, b):
    return lax.dot_general(a, b, (((1,), (1,)), ((), ())), preferred_element_type=F32)


def _dot_tn(a, b):
    return lax.dot_general(a, b, (((0,), (0,)), ((), ())), preferred_element_type=F32)


def _const_spec(shape):
    return pl.BlockSpec(shape, lambda *_: (0,) * len(shape))


def _inproj_body(x_ref, g_ref, w_ref, qc_ref, kas_ref, kaw_ref, gb_ref, wa2_ref, ba_ref,
                 q_o, ks_o, kw_o, vs_o, vw_o, kc_o, vc_o, gt_o, qb_o, kb_o, vb_o, rb_o, la_o, gm_o):
    h = _rms(x_ref[...], g_ref[...]).astype(BF16)

    def proj(lo, hi):
        return _dot(h, w_ref[:, lo:hi])

    q_o[...] = (proj(C_Q, C_KS) + qc_ref[...]).astype(BF16)
    ks_o[...] = (proj(C_KS, C_KW) + kas_ref[...]).astype(BF16)
    kw_o[...] = (proj(C_KW, C_VS) + kaw_ref[...]).astype(BF16)
    vs_o[...] = proj(C_VS, C_VW).astype(BF16)
    vw_o[...] = proj(C_VW, C_KC).astype(BF16)
    kc_o[...] = proj(C_KC, C_VC).astype(BF16)
    vc_o[...] = proj(C_VC, C_GT).astype(BF16)
    gt_o[...] = jax.nn.sigmoid(proj(C_GT, C_QB) + gb_ref[...]).astype(BF16)
    qb_o[...] = proj(C_QB, C_KB).astype(BF16)
    kb_o[...] = proj(C_KB, C_VB).astype(BF16)
    vb_o[...] = proj(C_VB, C_RB).astype(BF16)
    rb_o[...] = proj(C_RB, C_AL).astype(BF16)
    z = _dot(proj(C_AL, C_MG), wa2_ref[...], precision=HIGHEST) + ba_ref[...]
    log_sig = jnp.minimum(z, 0.0) - jnp.log1p(jnp.exp(-jnp.abs(z)))
    la_o[...] = log_sig * (1.0 / GLA_TAU)
    gm_o[...] = jax.nn.sigmoid(proj(C_MG, C_END)).astype(BF16)


def _inproj(x2, g, wcat, qconst, kaug_s, kaug_w, gate_b, wa2, ba):
    n = x2.shape[0]
    tiles_per_seq = SEQ // TM
    widths = [(C_KS - C_Q, BF16), (LANE * 2, BF16), (LANE * 2, BF16), (LANE * 2, BF16), (LANE * 2, BF16),
              (LANE, BF16), (LANE, BF16), (LANE, BF16), (GLA_KEY_WIDTH, BF16), (GLA_KEY_WIDTH, BF16),
              (GLA_VAL_WIDTH, BF16), (GLA_VAL_WIDTH, BF16), (GLA_KEY_WIDTH, F32), (2 * D_MODEL, BF16)]
    tile = lambda w: pl.BlockSpec((TM, w), lambda i: (i, 0))
    return pl.pallas_call(
        _inproj_body,
        grid=(n // TM,),
        in_specs=[tile(D_MODEL), _const_spec((1, D_MODEL)), _const_spec((D_MODEL, C_END)),
                  _const_spec((1, C_KS - C_Q)),
                  pl.BlockSpec((TM, 2 * LANE), lambda i: (i % tiles_per_seq, 0)),
                  pl.BlockSpec((TM, 2 * LANE), lambda i: (i % tiles_per_seq, 0)),
                  _const_spec((1, LANE)), _const_spec((LANE, GLA_KEY_WIDTH)), _const_spec((1, GLA_KEY_WIDTH))],
        out_specs=[tile(w) for w, _ in widths],
        out_shape=[jax.ShapeDtypeStruct((n, w), dt) for w, dt in widths],
        compiler_params=pltpu.CompilerParams(dimension_semantics=("arbitrary",), vmem_limit_bytes=VMEM_LIMIT),
        name="inproj",
    )(x2, g, wcat, qconst, kaug_s, kaug_w, gate_b, wa2, ba)


def _memkv_body(m_ref, g_ref, w_ref, k_o, v_o):
    hm = _rms(m_ref[0], g_ref[...]).astype(BF16)
    kv = _dot(hm, w_ref[...])
    k_o[0] = kv[:, :XATTN_WIDTH].astype(BF16)
    v_o[0] = kv[:, XATTN_WIDTH:].astype(BF16)


def _memkv(mem, g, w):
    b = mem.shape[0]
    blk = pl.BlockSpec((1, MEM_LEN, XATTN_WIDTH), lambda i: (i, 0, 0))
    return pl.pallas_call(
        _memkv_body,
        grid=(b,),
        in_specs=[pl.BlockSpec((1, MEM_LEN, D_MODEL), lambda i: (i, 0, 0)), _const_spec((1, D_MODEL)),
                  _const_spec((D_MODEL, 2 * XATTN_WIDTH))],
        out_specs=[blk, blk],
        out_shape=[jax.ShapeDtypeStruct((b, MEM_LEN, XATTN_WIDTH), BF16)] * 2,
        compiler_params=pltpu.CompilerParams(dimension_semantics=("arbitrary",)),
        name="memkv",
    )(mem, g, w)


def _compress(c_ref, wa_ref, wb_ref, w2_ref, p1_ref, p2_ref):
    c = c_ref[0].astype(F32)
    a = _dot((c + p1_ref[...]).astype(BF16), wa_ref[0])
    b = _dot((c + p2_ref[...]).astype(BF16), wb_ref[0])
    hid = a + pltpu.roll(b, N_CMP_PAD - 1, axis=0)
    return _dot(jax.nn.gelu(hid).astype(BF16), w2_ref[...]).astype(BF16)


def _softmax_rows(s, mask=None):
    m = jnp.max(s, axis=-1, keepdims=True)
    e = jnp.exp(s - m)
    if mask is not None:
        e = jnp.where(mask, e, 0.0)
    return e, jnp.sum(e, axis=-1, keepdims=True)


def _nsa_body(q_ref, ks_ref, kw_ref, vs_ref, vw_ref, kc_ref, vc_ref, gt_ref,
              wak_ref, wbk_ref, w2k_ref, wav_ref, wbv_ref, w2v_ref, p1k_ref, p2k_ref, p1v_ref, p2v_ref,
              rrep_ref, slope_ref, ovl_ref, o_ref,
              ocmp_ref, qs_ref, kwp_ref, vwp_ref):
    kcmp = _compress(kc_ref, wak_ref, wbk_ref, w2k_ref, p1k_ref, p2k_ref)
    vcmp = _compress(vc_ref, wav_ref, wbv_ref, w2v_ref, p1v_ref, p2v_ref)

    t_col = lax.broadcasted_iota(jnp.int32, (SEQ, LANE), 0)
    n_row = lax.broadcasted_iota(jnp.int32, (SEQ, LANE), 1)
    dist_c = (t_col - (n_row * CMP_STRIDE + (CMP_BLOCK - 1))).astype(F32)
    mask_c = dist_c >= 0.0
    low_half = n_row < HEAD_DIM
    imp = jnp.zeros((SEQ, LANE), F32)
    o_even = None
    for g in range(NSA_GROUP):
        qg = q_ref[0, :, g * LANE:(g + 1) * LANE]
        s = _dot_nt(qg, kcmp) - slope_ref[0, g:g + 1, :] * dist_c
        s = jnp.where(mask_c, s, NEG_INF)
        e, l = _softmax_rows(s, mask_c)
        p = e / jnp.where(l > 0.0, l, 1.0)
        imp = imp + p
        oc = _dot(p.astype(BF16), vcmp)
        if g % 2 == 0:
            o_even = oc
        else:
            ocmp_ref[g // 2] = jnp.where(low_half, o_even, oc)

    imp_b = _dot(ovl_ref[...], imp.T, precision=HIGHEST)
    j = lax.broadcasted_iota(jnp.int32, (N_SLC, SEQ), 0)
    cur = lax.broadcasted_iota(jnp.int32, (N_SLC, SEQ), 1) >> 6
    forced = (j == 0) | (j == cur) | (j == cur - 1)
    val = jnp.where(forced, float(NSA_GROUP + 1), jnp.where(j > cur, -1.0, imp_b))
    rank = jnp.zeros((N_SLC, SEQ), F32)
    for jp in range(N_SLC):
        vj = val[jp:jp + 1, :]
        beats = (vj > val) | ((vj == val) & (j > jp))
        rank = rank + jnp.where(beats, 1.0, 0.0)
    sel_bias = jnp.where(rank < float(N_SELECT), 0.0, NEG_INF)
    aug_t = jnp.concatenate([jnp.zeros((AUG_SEL, SEQ), F32), sel_bias,
                             jnp.zeros((LANE - AUG_SEL - N_SLC, SEQ), F32)], axis=0)
    aug = aug_t.T.astype(BF16)
    for g in range(NSA_GROUP):
        qs_ref[g] = q_ref[0, :, g * LANE:(g + 1) * LANE] + aug

    kwp_ref[0:WIN_PAD, :] = jnp.zeros((WIN_PAD, LANE), BF16)
    vwp_ref[0:WIN_PAD, :] = jnp.zeros((WIN_PAD, LANE), BF16)
    kwp_ref[WIN_PAD:, :] = kw_ref[0]
    vwp_ref[WIN_PAD:, :] = vw_ref[0]

    rows = NSA_GROUP * SLC_BLOCK
    lane64 = lax.broadcasted_iota(jnp.int32, (SLC_BLOCK, LANE), 1) < HEAD_DIM
    col_w = lax.broadcasted_iota(jnp.int32, (rows, WIN_KEYS), 1)
    qi_w = lax.broadcasted_iota(jnp.int32, (rows, WIN_KEYS), 0) & (SLC_BLOCK - 1)
    dist_w = qi_w + WIN_PAD - col_w
    band_w = (dist_w >= 0) & (dist_w < WINDOW)

    def qblock(i, nkeys):
        r0 = pl.multiple_of(i * SLC_BLOCK, SLC_BLOCK)
        qs = jnp.concatenate([qs_ref[g, pl.ds(r0, SLC_BLOCK), :] for g in range(NSA_GROUP)], axis=0)
        s = _dot_nt(qs, ks_ref[0, 0:nkeys, :])
        col = lax.broadcasted_iota(jnp.int32, (rows, nkeys), 1)
        qi = lax.broadcasted_iota(jnp.int32, (rows, nkeys), 0) & (SLC_BLOCK - 1)
        s = jnp.where(col <= r0 + qi, s, NEG_INF)
        e, l = _softmax_rows(s)
        o_s = _dot(e.astype(BF16), vs_ref[0, 0:nkeys, :]) / l
        qw = jnp.concatenate([q_ref[0, pl.ds(r0, SLC_BLOCK), g * LANE:(g + 1) * LANE]
                              for g in range(NSA_GROUP)], axis=0)
        sw = _dot_nt(qw, kwp_ref[pl.ds(r0, WIN_KEYS), :])
        mask_w = band_w & (col_w >= WIN_PAD - r0)
        sw = jnp.where(mask_w, sw, NEG_INF)
        ew, lw = _softmax_rows(sw, mask_w)
        o_w = _dot(ew.astype(BF16), vwp_ref[pl.ds(r0, WIN_KEYS), :]) / lw
        gt = _dot(gt_ref[0, pl.ds(r0, SLC_BLOCK), :], rrep_ref[0])
        for pp in range(2):
            a0, a1, a2 = 2 * pp * SLC_BLOCK, (2 * pp + 1) * SLC_BLOCK, (2 * pp + 2) * SLC_BLOCK
            osl = jnp.where(lane64, o_s[a0:a1], o_s[a1:a2])
            owi = jnp.where(lane64, o_w[a0:a1], o_w[a1:a2])
            out = (gt[:, pp * LANE:(pp + 1) * LANE] * ocmp_ref[pp, pl.ds(r0, SLC_BLOCK), :]
                   + gt[:, (2 + pp) * LANE:(3 + pp) * LANE] * osl
                   + gt[:, (4 + pp) * LANE:(5 + pp) * LANE] * owi)
            o_ref[0, pl.ds(r0, SLC_BLOCK), pp * LANE:(pp + 1) * LANE] = out.astype(BF16)

    n_buckets = 4
    per = N_SLC // n_buckets
    for kb in range(n_buckets):
        nkeys = (kb + 1) * per * SLC_BLOCK

        def body(i, carry, nkeys=nkeys):
            qblock(i, nkeys)
            return carry

        lax.fori_loop(kb * per, (kb + 1) * per, body, 0)


def _nsa(q, ks, kw, vs, vw, kc2, vc2, gt, wak, wbk, w2k, wav, wbv, w2v, p1k, p2k, p1v, p2v, rrep, slopes, ovl):
    b = q.shape[0]
    hs = lambda w: pl.BlockSpec((1, SEQ, w), lambda i, h: (i, 0, h))
    perh = lambda s: pl.BlockSpec((1,) + s, lambda i, h: (h,) + (0,) * len(s))
    cblk = pl.BlockSpec((1, N_CMP_PAD, CMP_STRIDE * 2 * HEAD_DIM), lambda i, h: (i, 0, 0))
    w1 = (CMP_STRIDE * 2 * HEAD_DIM, CMP_HIDDEN)
    return pl.pallas_call(
        _nsa_body,
        grid=(b, NSA_KV_HEADS),
        in_specs=[hs(NSA_GROUP * LANE), hs(LANE), hs(LANE), hs(LANE), hs(LANE), cblk, cblk,
                  pl.BlockSpec((1, SEQ, LANE), lambda i, h: (i, 0, 0)),
                  perh(w1), perh(w1), _const_spec((CMP_HIDDEN, LANE)),
                  perh(w1), perh(w1), _const_spec((CMP_HIDDEN, LANE)),
                  _const_spec((1, w1[0])), _const_spec((1, w1[0])), _const_spec((1, w1[0])), _const_spec((1, w1[0])),
                  perh((LANE, 6 * LANE)), perh((8, LANE)), _const_spec((N_SLC, LANE))],
        out_specs=pl.BlockSpec((1, SEQ, 2 * LANE), lambda i, h: (i, 0, h)),
        out_shape=jax.ShapeDtypeStruct((b, SEQ, NSA_HEADS * HEAD_DIM), BF16),
        scratch_shapes=[pltpu.VMEM((2, SEQ, LANE), F32), pltpu.VMEM((NSA_GROUP, SEQ, LANE), BF16),
                        pltpu.VMEM((SEQ + WIN_PAD, LANE), BF16), pltpu.VMEM((SEQ + WIN_PAD, LANE), BF16)],
        compiler_params=pltpu.CompilerParams(dimension_semantics=("arbitrary", "arbitrary"),
                                             vmem_limit_bytes=VMEM_LIMIT),
        name="nsa",
    )(q, ks, kw, vs, vw, kc2, vc2, gt, wak, wbk, w2k, wav, wbv, w2v, p1k, p2k, p1v, p2v, rrep, slopes, ovl)


def _gla_body(q_ref, k_ref, v_ref, r_ref, la_ref, ng_ref, o_ref):
    ii = lax.broadcasted_iota(jnp.int32, (GLA_CHUNK, GLA_CHUNK), 0)
    jj = lax.broadcasted_iota(jnp.int32, (GLA_CHUNK, GLA_CHUNK), 1)
    causal = jj <= ii
    tril = jnp.where(causal, 1.0, 0.0).astype(F32)
    head_q = lax.broadcasted_iota(jnp.int32, (GLA_CHUNK, GLA_KEY_WIDTH), 1) >> 6
    head_s = lax.broadcasted_iota(jnp.int32, (GLA_DV, GLA_KEY_WIDTH), 1) >> 6

    def chunk(n, state_t):
        r0 = pl.multiple_of(n * GLA_CHUNK, GLA_CHUNK)
        rows = pl.ds(r0, GLA_CHUNK)
        b = _dot(tril, la_ref[0, rows, :], precision=HIGHEST)
        b_last = b[GLA_CHUNK - 1:GLA_CHUNK, :]
        q = q_ref[0, rows, :].astype(F32)
        k = k_ref[0, rows, :].astype(F32)
        q_d = q * jnp.exp(b)
        k_d = (k * jnp.exp(-b)).astype(BF16)
        k_s = (k * jnp.exp(b_last - b)).astype(BF16)
        v = v_ref[0, rows, :]
        inc_full = _dot_tn(v, k_s)
        state_b = state_t.astype(BF16)
        inc = jnp.zeros((GLA_DV, GLA_KEY_WIDTH), F32)
        for h in range(GLA_HEADS):
            qm = jnp.where(head_q == h, q_d, 0.0).astype(BF16)
            att = jnp.where(causal, _dot_nt(qm, k_d), 0.0)
            o = _dot(att.astype(BF16), v[:, h * GLA_DV:(h + 1) * GLA_DV]) + _dot_nt(qm, state_b)
            o = o * lax.rsqrt(jnp.mean(o * o, axis=-1, keepdims=True) + RMS_EPS) * ng_ref[...]
            r = r_ref[0, rows, h * GLA_DV:(h + 1) * GLA_DV].astype(F32)
            o_ref[0, rows, h * GLA_DV:(h + 1) * GLA_DV] = (o * (r * jax.nn.sigmoid(r))).astype(BF16)
            inc = inc + jnp.where(head_s == h, inc_full[h * GLA_DV:(h + 1) * GLA_DV, :], 0.0)
        return state_t * jnp.exp(b_last) + inc

    lax.fori_loop(0, SEQ // GLA_CHUNK, chunk, jnp.zeros((GLA_DV, GLA_KEY_WIDTH), F32))


def _gla(qb, kb, vb, rb, la, ng):
    b = qb.shape[0]
    blk = lambda w: pl.BlockSpec((1, SEQ, w), lambda i: (i, 0, 0))
    return pl.pallas_call(
        _gla_body,
        grid=(b,),
        in_specs=[blk(GLA_KEY_WIDTH), blk(GLA_KEY_WIDTH), blk(GLA_VAL_WIDTH), blk(GLA_VAL_WIDTH),
                  blk(GLA_KEY_WIDTH), _const_spec((1, GLA_DV))],
        out_specs=blk(GLA_VAL_WIDTH),
        out_shape=jax.ShapeDtypeStruct((b, SEQ, GLA_VAL_WIDTH), BF16),
        compiler_params=pltpu.CompilerParams(dimension_semantics=("arbitrary",), vmem_limit_bytes=VMEM_LIMIT),
        name="gla",
    )(qb, kb, vb, rb, la, ng)


def _mixx_body(x_ref, on_ref, og_ref, gm_ref, wn_ref, wg_ref, wo_ref, lx_ref, wq_ref, km_ref, vm_ref, wxo_ref,
               o_ref):
    gm = gm_ref[...].astype(F32)
    mix = (gm[:, :D_MODEL] * _dot(on_ref[...], wn_ref[...])
           + gm[:, D_MODEL:] * _dot(og_ref[...], wg_ref[...]))
    x1 = x_ref[...] + _dot(mix.astype(BF16), wo_ref[...])
    hx = _rms(x1, lx_ref[...]).astype(BF16)
    q = (_dot(hx, wq_ref[...]) * (XATTN_HEAD_DIM ** -0.5)).astype(BF16)
    outs = []
    for h in range(XATTN_HEADS):
        sl = slice(h * XATTN_HEAD_DIM, (h + 1) * XATTN_HEAD_DIM)
        e, l = _softmax_rows(_dot_nt(q[:, sl], km_ref[0, :, sl]))
        outs.append((_dot(e.astype(BF16), vm_ref[0, :, sl]) / l).astype(BF16))
    o = jnp.concatenate(outs, axis=-1)
    o_ref[...] = x1 + _dot(o, wxo_ref[...])


def _mixx(x2, o_nsa, o_gla, gm, wn, wg, wo, lx, wq, km, vm, wxo):
    n = x2.shape[0]
    tiles_per_seq = SEQ // TM
    tile = lambda w: pl.BlockSpec((TM, w), lambda i: (i, 0))
    mblk = pl.BlockSpec((1, MEM_LEN, XATTN_WIDTH), lambda i: (i // tiles_per_seq, 0, 0))
    return pl.pallas_call(
        _mixx_body,
        grid=(n // TM,),
        in_specs=[tile(D_MODEL), tile(XATTN_WIDTH), tile(GLA_VAL_WIDTH), tile(2 * D_MODEL),
                  _const_spec((XATTN_WIDTH, D_MODEL)), _const_spec((GLA_VAL_WIDTH, D_MODEL)),
                  _const_spec((D_MODEL, D_MODEL)), _const_spec((1, D_MODEL)),
                  _const_spec((D_MODEL, XATTN_WIDTH)), mblk, mblk, _const_spec((XATTN_WIDTH, D_MODEL))],
        out_specs=tile(D_MODEL),
        out_shape=jax.ShapeDtypeStruct((n, D_MODEL), F32),
        compiler_params=pltpu.CompilerParams(dimension_semantics=("arbitrary",), vmem_limit_bytes=VMEM_LIMIT),
        name="mixx",
    )(x2, o_nsa, o_gla, gm, wn, wg, wo, lx, wq, km, vm, wxo)


def _ffn_body(x_ref, lf_ref, wu_ref, wgt_ref, cw_ref, cb_ref, wd_ref, lfin_ref, o_ref, ubuf_ref, carry_ref):
    @pl.when(pl.program_id(0) % (SEQ // TM) == 0)
    def _():
        carry_ref[...] = jnp.zeros_like(carry_ref)

    x = x_ref[...]
    hf = _rms(x, lf_ref[...]).astype(BF16)
    acc = jnp.zeros((TM, D_MODEL), F32)
    for c in range(FFN_DIM // FFN_CHUNK):
        cols = slice(c * FFN_CHUNK, (c + 1) * FFN_CHUNK)
        ubuf_ref[0:8, :] = carry_ref[:, cols]
        ubuf_ref[8:, :] = _dot(hf, wu_ref[:, cols])
        carry_ref[:, cols] = ubuf_ref[TM:TM + 8, :]
        u = (cw_ref[0:1, cols] * ubuf_ref[6:6 + TM, :] + cw_ref[1:2, cols] * ubuf_ref[7:7 + TM, :]
             + cw_ref[2:3, cols] * ubuf_ref[8:8 + TM, :] + cb_ref[:, cols])
        act = (jax.nn.gelu(u) * _dot(hf, wgt_ref[:, cols])).astype(BF16)
        acc = acc + _dot(act, wd_ref[cols, :])
    o_ref[...] = _rms(x + acc, lfin_ref[...])


def _ffn(x2, lf, wu, wgt, cw, cb, wd, lfin):
    n = x2.shape[0]
    tile = pl.BlockSpec((TM, D_MODEL), lambda i: (i, 0))
    return pl.pallas_call(
        _ffn_body,
        grid=(n // TM,),
        in_specs=[tile, _const_spec((1, D_MODEL)), _const_spec((D_MODEL, FFN_DIM)), _const_spec((D_MODEL, FFN_DIM)),
                  _const_spec((CONV_WIDTH, FFN_DIM)), _const_spec((1, FFN_DIM)), _const_spec((FFN_DIM, D_MODEL)),
                  _const_spec((1, D_MODEL))],
        out_specs=tile,
        out_shape=jax.ShapeDtypeStruct((n, D_MODEL), F32),
        scratch_shapes=[pltpu.VMEM((TM + 8, FFN_CHUNK), F32), pltpu.VMEM((8, FFN_DIM), F32)],
        compiler_params=pltpu.CompilerParams(dimension_semantics=("arbitrary",), vmem_limit_bytes=VMEM_LIMIT),
        name="ffn",
    )(x2, lf, wu, wgt, cw, cb, wd, lfin)


def _layout_inproj(w_in):
    o = 0
    seg = {}
    for name, wdt in (("q", 512), ("kc", 128), ("vc", 128), ("ks", 128), ("vs", 128), ("kw", 128), ("vw", 128),
                      ("gt", 24), ("qb", 256), ("kb", 256), ("vb", 512), ("rb", 512), ("al", 16), ("mg", 2048)):
        seg[name] = w_in[:, o:o + wdt]
        o += wdt
    z64 = jnp.zeros((D_MODEL, HEAD_DIM), F32)
    head = lambda w, h: w[:, h * HEAD_DIM:(h + 1) * HEAD_DIM]
    cols = [jnp.concatenate([head(seg["q"], hh) * (HEAD_DIM ** -0.5), z64], 1) for hh in range(NSA_HEADS)]
    cols += [jnp.concatenate([head(seg["ks"], h), z64], 1) for h in range(NSA_KV_HEADS)]
    cols += [jnp.concatenate([head(seg["kw"], h), z64], 1) for h in range(NSA_KV_HEADS)]
    cols += [jnp.concatenate([head(seg["vs"], h)] * 2, 1) for h in range(NSA_KV_HEADS)]
    cols += [jnp.concatenate([head(seg["vw"], h)] * 2, 1) for h in range(NSA_KV_HEADS)]
    cols += [seg["kc"], seg["vc"]]
    cols += [jnp.pad(seg["gt"], ((0, 0), (0, LANE - 3 * NSA_HEADS)))]
    cols += [seg["qb"] * (GLA_DK ** -0.5), seg["kb"], seg["vb"], seg["rb"]]
    cols += [jnp.pad(seg["al"], ((0, 0), (0, LANE - GLA_RANK)))]
    cols += [seg["mg"]]
    return jnp.concatenate(cols, axis=1).astype(BF16)


def _position_constants():
    slopes = 2.0 ** (-np.arange(1, NSA_HEADS + 1, dtype=np.float64))
    qconst = np.zeros((1, NSA_HEADS * LANE), np.float32)
    for hh in range(NSA_HEADS):
        qconst[0, hh * LANE + AUG_HI] = slopes[hh]
        qconst[0, hh * LANE + AUG_LO] = slopes[hh]
    t = np.arange(SEQ)
    kaug_s = np.zeros((SEQ, 2 * LANE), np.float32)
    kaug_w = np.zeros((SEQ, 2 * LANE), np.float32)
    for h in range(NSA_KV_HEADS):
        kaug_s[t, h * LANE + AUG_SEL + t // SLC_BLOCK] = 1.0
        for a in (kaug_s, kaug_w):
            a[:, h * LANE + AUG_HI] = (t // SLC_BLOCK) * SLC_BLOCK
            a[:, h * LANE + AUG_LO] = t % SLC_BLOCK
    slope_rows = np.zeros((NSA_KV_HEADS, 8, LANE), np.float32)
    rrep = np.zeros((NSA_KV_HEADS, LANE, 6 * LANE), np.float32)
    for h in range(NSA_KV_HEADS):
        for g in range(NSA_GROUP):
            slope_rows[h, g, :] = slopes[h * NSA_GROUP + g]
            for c in range(3):
                pp, half = g // 2, g % 2
                c0 = (c * 2 + pp) * LANE + half * HEAD_DIM
                rrep[h, (h * NSA_GROUP + g) * 3 + c, c0:c0 + HEAD_DIM] = 1.0
    ovl = np.zeros((N_SLC, N_CMP_PAD), np.float32)
    for n in range(N_CMP_PAD - 1):
        for tok in range(n * CMP_STRIDE, n * CMP_STRIDE + CMP_BLOCK):
            ovl[tok // SLC_BLOCK, n] += 1.0 / CMP_BLOCK
    return qconst, kaug_s, kaug_w, slope_rows, rrep, ovl


def _layout_compress(w1, w2, pos, dup):
    w1 = w1.reshape(2, CMP_STRIDE, 1, HEAD_DIM, CMP_HIDDEN)
    sel = jnp.eye(NSA_KV_HEADS, dtype=F32)[:, None, None, :, None, None]
    wh = (w1[None] * sel).reshape(NSA_KV_HEADS, 2, CMP_STRIDE * NSA_KV_HEADS * HEAD_DIM, CMP_HIDDEN)
    p = jnp.broadcast_to(pos.reshape(2, CMP_STRIDE, 1, HEAD_DIM), (2, CMP_STRIDE, NSA_KV_HEADS, HEAD_DIM))
    p = p.reshape(2, 1, CMP_STRIDE * NSA_KV_HEADS * HEAD_DIM)
    w2p = jnp.concatenate([w2, w2 if dup else jnp.zeros_like(w2)], axis=1)
    return wh[:, 0].astype(BF16), wh[:, 1].astype(BF16), w2p.astype(BF16), p[0], p[1]


def kernel(x, mem, ln_mix_g, w_in, nsa_gate_b, cmp_pos_k, cmp_w1_k, cmp_w2_k, cmp_pos_v, cmp_w1_v, cmp_w2_v,
           gla_w_alpha2, gla_b_alpha, gla_norm_g, w_branch_nsa, w_branch_gla, w_out, ln_x_g, ln_mem_g, w_xq,
           w_xkv, w_xo, ln_ffn_g, w_up, conv_w, conv_b, w_down, ln_final_g):
    b = x.shape[0]
    n = b * SEQ
    row = lambda v: v.reshape(1, -1).astype(F32)
    qconst, kaug_s, kaug_w, slope_rows, rrep, ovl = _position_constants()

    x2 = x.reshape(n, D_MODEL)
    gate_b = jnp.pad(nsa_gate_b[0], (0, LANE - 3 * NSA_HEADS)).reshape(1, LANE)
    wa2 = jnp.pad(gla_w_alpha2[0], ((0, LANE - GLA_RANK), (0, 0)))
    (q, ks, kw, vs, vw, kc, vc, gt, qb, kb, vb, rb, la, gm) = _inproj(
        x2, row(ln_mix_g[0]), _layout_inproj(w_in[0]), qconst, kaug_s, kaug_w, gate_b, wa2, row(gla_b_alpha[0]))

    s3 = lambda a: a.reshape(b, SEQ, a.shape[-1])
    chunked = lambda a: a.reshape(b, N_CMP_PAD, CMP_STRIDE * NSA_KV_HEADS * HEAD_DIM)
    wak, wbk, w2k, p1k, p2k = _layout_compress(cmp_w1_k[0], cmp_w2_k[0], cmp_pos_k[0], dup=False)
    wav, wbv, w2v, p1v, p2v = _layout_compress(cmp_w1_v[0], cmp_w2_v[0], cmp_pos_v[0], dup=True)
    o_nsa = _nsa(s3(q), s3(ks), s3(kw), s3(vs), s3(vw), chunked(kc), chunked(vc), s3(gt),
                 wak, wbk, w2k, wav, wbv, w2v, p1k, p2k, p1v, p2v,
                 jnp.asarray(rrep, BF16), jnp.asarray(slope_rows), jnp.asarray(ovl))

    o_gla = _gla(s3(qb), s3(kb), s3(vb), s3(rb), s3(la), row(gla_norm_g[0]))

    km, vm = _memkv(mem, row(ln_mem_g[0]), w_xkv[0].astype(BF16))
    x_mid = _mixx(x2, o_nsa.reshape(n, -1), o_gla.reshape(n, -1), gm,
                  w_branch_nsa[0].astype(BF16), w_branch_gla[0].astype(BF16), w_out[0].astype(BF16),
                  row(ln_x_g[0]), w_xq[0].astype(BF16), km, vm, w_xo[0].astype(BF16))

    out = _ffn(x_mid, row(ln_ffn_g[0]), w_up[0][:, :FFN_DIM].astype(BF16), w_up[0][:, FFN_DIM:].astype(BF16),
               conv_w[0], conv_b[0].reshape(1, FFN_DIM), w_down[0].astype(BF16), row(ln_final_g))
    return out.reshape(b, SEQ, D_MODEL)
```

```python
import math

import jax
import jax.numpy as jnp
import numpy as np
from jax import lax
from jax.experimental import pallas as pl
from jax.experimental.pallas import tpu as pltpu

F32 = jnp.float32
BF16 = jnp.bfloat16

D_MODEL = 1024
SEQ = 2048
MEM_LEN = 256
HEAD_DIM = 64
NSA_HEADS = 8
NSA_KV_HEADS = 2
NSA_GROUP = 4
CMP_BLOCK = 32
CMP_STRIDE = 16
CMP_HIDDEN = 128
N_CMP_PAD = SEQ // CMP_STRIDE
SLC_BLOCK = 64
N_SLC = SEQ // SLC_BLOCK
N_SELECT = 8
WINDOW = 256
GLA_HEADS = 4
GLA_DK = 64
GLA_DV = 128
GLA_KEY_WIDTH = 256
GLA_VAL_WIDTH = 512
GLA_RANK = 16
GLA_TAU = 16.0
GLA_CHUNK = 64
GLA_GROUP = 256
XATTN_HEADS = 4
XATTN_HEAD_DIM = 128
XATTN_WIDTH = 512
FFN_DIM = 2816
CONV_WIDTH = 3
RMS_EPS = 1e-6
NEG_INF = -1e30
LOG2E = math.log2(math.e)

LANE = 128
VMEM_LIMIT = 56 * 1024 * 1024

AUG_SEL = HEAD_DIM
AUG_HI = HEAD_DIM + N_SLC
AUG_LO = AUG_HI + 1

C_Q = 0
C_KS = C_Q + NSA_HEADS * LANE
C_KW = C_KS + NSA_KV_HEADS * LANE
C_VS = C_KW + NSA_KV_HEADS * LANE
C_VW = C_VS + NSA_KV_HEADS * LANE
C_KC = C_VW + NSA_KV_HEADS * LANE
C_VC = C_KC + LANE
C_GT = C_VC + LANE
C_QB = C_GT + LANE
C_KB = C_QB + GLA_KEY_WIDTH
C_VB = C_KB + GLA_KEY_WIDTH
C_RB = C_VB + GLA_VAL_WIDTH
C_AL = C_RB + GLA_VAL_WIDTH
C_MG = C_AL + LANE
C_END = C_MG + 2 * D_MODEL

TM = 512
WIN_PAD = 320
WIN_KEYS = WIN_PAD + SLC_BLOCK
QROWS = NSA_GROUP * SLC_BLOCK
FFN_CHUNK = 1408


def _rms(x, g):
    return x * lax.rsqrt(jnp.mean(x * x, axis=-1, keepdims=True) + RMS_EPS) * g


def _dot(a, b, **kw):
    return jnp.dot(a, b, preferred_element_type=F32, **kw)


def _dot_nt(a, b):
    return lax.dot_general(a, b, (((1,), (1,)), ((), ())), preferred_element_type=F32)


def _dot_tn(a, b):
    return lax.dot_general(a, b, (((0,), (0,)), ((), ())), preferred_element_type=F32)


def _split_bf16(x):
    hi = x.astype(BF16)
    return hi, (x - hi.astype(F32)).astype(BF16)


def _const_spec(shape):
    return pl.BlockSpec(shape, lambda *_: (0,) * len(shape))


def _inproj_body(x_ref, g_ref, w_ref, qc_ref, kas_ref, kaw_ref, gb_ref, wa2_ref, ba_ref,
                 q_o, ks_o, kw_o, vs_o, vw_o, kc_o, vc_o, gt_o, qb_o, kb_o, vb_o, rb_o, la_o, gm_o):
    h = _rms(x_ref[...], g_ref[...]).astype(BF16)

    def proj(lo, hi):
        return _dot(h, w_ref[:, lo:hi])

    q_o[...] = (proj(C_Q, C_KS) + qc_ref[...]).astype(BF16)
    ks_o[...] = (proj(C_KS, C_KW) + kas_ref[...]).astype(BF16)
    kw_o[...] = (proj(C_KW, C_VS) + kaw_ref[...]).astype(BF16)
    vs_o[...] = proj(C_VS, C_VW).astype(BF16)
    vw_o[...] = proj(C_VW, C_KC).astype(BF16)
    kc_o[...] = proj(C_KC, C_VC).astype(BF16)
    vc_o[...] = proj(C_VC, C_GT).astype(BF16)
    gt_o[...] = jax.nn.sigmoid(proj(C_GT, C_QB) + gb_ref[...]).astype(BF16)
    qb_o[...] = proj(C_QB, C_KB).astype(BF16)
    kb_o[...] = proj(C_KB, C_VB).astype(BF16)
    vb_o[...] = proj(C_VB, C_RB).astype(BF16)
    rb_o[...] = proj(C_RB, C_AL).astype(BF16)
    z = _dot(proj(C_AL, C_MG), wa2_ref[...], precision=lax.Precision.HIGHEST) + ba_ref[...]
    log_sig = jnp.minimum(z, 0.0) - jnp.log1p(jnp.exp(-jnp.abs(z)))
    la_o[...] = log_sig * (1.0 / GLA_TAU)
    gm_o[...] = jax.nn.sigmoid(proj(C_MG, C_END)).astype(BF16)


def _inproj(x2, g, wcat, qconst, kaug_s, kaug_w, gate_b, wa2, ba):
    n = x2.shape[0]
    tiles_per_seq = SEQ // TM
    widths = [(C_KS - C_Q, BF16), (LANE * 2, BF16), (LANE * 2, BF16), (LANE * 2, BF16), (LANE * 2, BF16),
              (LANE, BF16), (LANE, BF16), (LANE, BF16), (GLA_KEY_WIDTH, BF16), (GLA_KEY_WIDTH, BF16),
              (GLA_VAL_WIDTH, BF16), (GLA_VAL_WIDTH, BF16), (GLA_KEY_WIDTH, F32), (2 * D_MODEL, BF16)]
    tile = lambda w: pl.BlockSpec((TM, w), lambda i: (i, 0))
    return pl.pallas_call(
        _inproj_body,
        grid=(n // TM,),
        in_specs=[tile(D_MODEL), _const_spec((1, D_MODEL)), _const_spec((D_MODEL, C_END)),
                  _const_spec((1, C_KS - C_Q)),
                  pl.BlockSpec((TM, 2 * LANE), lambda i: (i % tiles_per_seq, 0)),
                  pl.BlockSpec((TM, 2 * LANE), lambda i: (i % tiles_per_seq, 0)),
                  _const_spec((1, LANE)), _const_spec((LANE, GLA_KEY_WIDTH)), _const_spec((1, GLA_KEY_WIDTH))],
        out_specs=[tile(w) for w, _ in widths],
        out_shape=[jax.ShapeDtypeStruct((n, w), dt) for w, dt in widths],
        compiler_params=pltpu.CompilerParams(dimension_semantics=("arbitrary",), vmem_limit_bytes=VMEM_LIMIT),
        name="inproj",
    )(x2, g, wcat, qconst, kaug_s, kaug_w, gate_b, wa2, ba)


def _memkv_body(m_ref, g_ref, w_ref, k_o, v_o):
    hm = _rms(m_ref[0], g_ref[...]).astype(BF16)
    kv = _dot(hm, w_ref[...])
    k_o[0] = kv[:, :XATTN_WIDTH].astype(BF16)
    v_o[0] = kv[:, XATTN_WIDTH:].astype(BF16)


def _memkv(mem, g, w):
    b = mem.shape[0]
    blk = pl.BlockSpec((1, MEM_LEN, XATTN_WIDTH), lambda i: (i, 0, 0))
    return pl.pallas_call(
        _memkv_body,
        grid=(b,),
        in_specs=[pl.BlockSpec((1, MEM_LEN, D_MODEL), lambda i: (i, 0, 0)), _const_spec((1, D_MODEL)),
                  _const_spec((D_MODEL, 2 * XATTN_WIDTH))],
        out_specs=[blk, blk],
        out_shape=[jax.ShapeDtypeStruct((b, MEM_LEN, XATTN_WIDTH), BF16)] * 2,
        compiler_params=pltpu.CompilerParams(dimension_semantics=("arbitrary",)),
        name="memkv",
    )(mem, g, w)


def _compress(c_ref, wa_ref, wb_ref, w2_ref, p1_ref, p2_ref):
    c = c_ref[0].astype(F32)
    a = _dot((c + p1_ref[...]).astype(BF16), wa_ref[0])
    b = _dot((c + p2_ref[...]).astype(BF16), wb_ref[0])
    hid = a + pltpu.roll(b, N_CMP_PAD - 1, axis=0)
    return _dot(jax.nn.gelu(hid).astype(BF16), w2_ref[...]).astype(BF16)


def _nsa_constants(slope_ref, biasc_ref, wbias_ref, dbias_ref):
    t_col = lax.broadcasted_iota(jnp.int32, (SEQ, LANE), 0)
    n_row = lax.broadcasted_iota(jnp.int32, (SEQ, LANE), 1)
    dist_c = (t_col - (n_row * CMP_STRIDE + (CMP_BLOCK - 1))).astype(F32)
    for g in range(NSA_GROUP):
        biasc_ref[g] = jnp.where(dist_c >= 0.0, -slope_ref[0, g:g + 1, :] * dist_c, NEG_INF)
    col = lax.broadcasted_iota(jnp.int32, (QROWS, WIN_KEYS), 1)
    qi = lax.broadcasted_iota(jnp.int32, (QROWS, WIN_KEYS), 0) & (SLC_BLOCK - 1)
    dist_w = qi + WIN_PAD - col
    wbias_ref[...] = jnp.where((dist_w >= 0) & (dist_w < WINDOW), 0.0, NEG_INF)
    row = lax.broadcasted_iota(jnp.int32, (2 * QROWS, LANE), 0)
    colp = lax.broadcasted_iota(jnp.int32, (2 * QROWS, LANE), 1)
    same_blk = (row >> 8) == (colp >> 6)
    causal = (colp & (SLC_BLOCK - 1)) <= (row & (SLC_BLOCK - 1))
    dbias_ref[...] = jnp.where(same_blk & causal, 0.0, NEG_INF)


def _nsa_body(q_ref, ks_ref, kw_ref, vs_ref, vw_ref, kc_ref, vc_ref, gt_ref,
              wak_ref, wbk_ref, w2k_ref, wav_ref, wbv_ref, w2v_ref, p1k_ref, p2k_ref, p1v_ref, p2v_ref,
              rrep_ref, slope_ref, ovl_ref, o_ref,
              ocmp_ref, qs_ref, kwp_ref, vwp_ref, biasc_ref, wbias_ref, dbias_ref):
    @pl.when(pl.program_id(1) == 0)
    def _():
        _nsa_constants(slope_ref, biasc_ref, wbias_ref, dbias_ref)

    kcmp = _compress(kc_ref, wak_ref, wbk_ref, w2k_ref, p1k_ref, p2k_ref)
    vcmp = _compress(vc_ref, wav_ref, wbv_ref, w2v_ref, p1v_ref, p2v_ref)

    low_half = lax.broadcasted_iota(jnp.int32, (SEQ, LANE), 1) < HEAD_DIM
    has_key = lax.broadcasted_iota(jnp.int32, (SEQ, 1), 0) >= CMP_BLOCK - 1
    imp = jnp.zeros((SEQ, LANE), F32)
    o_even = None
    for g in range(NSA_GROUP):
        s = _dot_nt(q_ref[0, :, g * LANE:(g + 1) * LANE], kcmp) + biasc_ref[g]
        e = jnp.exp2(s - jnp.max(s, axis=-1, keepdims=True))
        l = jnp.sum(e, axis=-1, keepdims=True)
        p = e * jnp.where(has_key, 1.0 / l, 0.0)
        imp = imp + p
        oc = _dot(p.astype(BF16), vcmp)
        if g % 2 == 0:
            o_even = oc
        else:
            ocmp_ref[g // 2] = jnp.where(low_half, o_even, oc)

    hi, lo = _split_bf16(imp.T)
    ovl = ovl_ref[...]
    imp_b = _dot(ovl, hi) + _dot(ovl, lo)
    j = lax.broadcasted_iota(jnp.int32, (N_SLC, SEQ), 0)
    cur = lax.broadcasted_iota(jnp.int32, (N_SLC, SEQ), 1) >> 6
    forced = (j == 0) | (j == cur) | (j == cur - 1)
    val = jnp.where(forced, float(NSA_GROUP + 1), jnp.where(j > cur, -1.0, imp_b))
    j8 = lax.broadcasted_iota(jnp.int32, (8, SEQ), 0)
    vals = [val[8 * r:8 * r + 8, :] for r in range(N_SLC // 8)]
    ranks = [jnp.zeros((8, SEQ), F32) for _ in vals]
    for jp in range(N_SLC):
        vj = val[jp:jp + 1, :]
        for r, vr in enumerate(vals):
            if 8 * r + 7 <= jp:
                beats = vj > vr
            elif 8 * r > jp:
                beats = vj >= vr
            else:
                beats = (vj > vr) | ((vj == vr) & (j8 > jp - 8 * r))
            ranks[r] = ranks[r] + jnp.where(beats, 1.0, 0.0)
    rank = jnp.concatenate(ranks, axis=0)
    sel_bias = jnp.where((rank < float(N_SELECT)) & (j < cur), 0.0, NEG_INF)
    aug_t = jnp.concatenate([jnp.zeros((AUG_SEL, SEQ), F32), sel_bias,
                             jnp.zeros((LANE - AUG_SEL - N_SLC, SEQ), F32)], axis=0)
    aug = aug_t.T.astype(BF16)
    for g in range(NSA_GROUP):
        qs_ref[g] = q_ref[0, :, g * LANE:(g + 1) * LANE] + aug

    pad_lane = lax.broadcasted_iota(jnp.int32, (WIN_PAD, LANE), 1)
    kwp_ref[0:WIN_PAD, :] = jnp.where(pad_lane == AUG_HI, NEG_INF, 0.0).astype(BF16)
    vwp_ref[0:WIN_PAD, :] = jnp.zeros((WIN_PAD, LANE), BF16)
    kwp_ref[WIN_PAD:, :] = kw_ref[0]
    vwp_ref[WIN_PAD:, :] = vw_ref[0]

    lane64 = lax.broadcasted_iota(jnp.int32, (SLC_BLOCK, LANE), 1) < HEAD_DIM

    def q_rows(r0, with_selection):
        if with_selection:
            parts = [qs_ref[g, pl.ds(r0, SLC_BLOCK), :] for g in range(NSA_GROUP)]
        else:
            parts = [q_ref[0, pl.ds(r0, SLC_BLOCK), g * LANE:(g + 1) * LANE] for g in range(NSA_GROUP)]
        return jnp.concatenate(parts, axis=0)

    def finish_block(r0, qp, o_s):
        sw = _dot_nt(qp, kwp_ref[pl.ds(r0, WIN_KEYS), :]) + wbias_ref[...]
        ew = jnp.exp2(sw - jnp.max(sw, axis=-1, keepdims=True))
        lw = jnp.sum(ew, axis=-1, keepdims=True)
        o_w = _dot(ew.astype(BF16), vwp_ref[pl.ds(r0, WIN_KEYS), :]) * (1.0 / lw)
        gt = _dot(gt_ref[0, pl.ds(r0, SLC_BLOCK), :], rrep_ref[0])
        for pp in range(2):
            a0, a1, a2 = 2 * pp * SLC_BLOCK, (2 * pp + 1) * SLC_BLOCK, (2 * pp + 2) * SLC_BLOCK
            osl = jnp.where(lane64, o_s[a0:a1], o_s[a1:a2])
            owi = jnp.where(lane64, o_w[a0:a1], o_w[a1:a2])
            out = (gt[:, pp * LANE:(pp + 1) * LANE] * ocmp_ref[pp, pl.ds(r0, SLC_BLOCK), :]
                   + gt[:, (2 + pp) * LANE:(3 + pp) * LANE] * osl
                   + gt[:, (4 + pp) * LANE:(5 + pp) * LANE] * owi)
            o_ref[0, pl.ds(r0, SLC_BLOCK), pp * LANE:(pp + 1) * LANE] = out.astype(BF16)

    def qpair(ip, nkeys):
        r0 = pl.multiple_of(ip * (2 * SLC_BLOCK), 2 * SLC_BLOCK)
        r1 = r0 + SLC_BLOCK
        qa, qb = q_rows(r0, False), q_rows(r1, False)
        qs = jnp.concatenate([q_rows(r0, True), q_rows(r1, True)], axis=0)
        s_p = _dot_nt(qs, ks_ref[0, 0:nkeys, :])
        s_d = _dot_nt(jnp.concatenate([qa, qb], axis=0), ks_ref[0, pl.ds(r0, 2 * SLC_BLOCK), :]) + dbias_ref[...]
        m = jnp.maximum(jnp.max(s_p, axis=-1, keepdims=True), jnp.max(s_d, axis=-1, keepdims=True))
        e_p = jnp.exp2(s_p - m)
        e_d = jnp.exp2(s_d - m)
        l = jnp.sum(e_p, axis=-1, keepdims=True) + jnp.sum(e_d, axis=-1, keepdims=True)
        o_s = (_dot(e_p.astype(BF16), vs_ref[0, 0:nkeys, :])
               + _dot(e_d.astype(BF16), vs_ref[0, pl.ds(r0, 2 * SLC_BLOCK), :])) * (1.0 / l)
        finish_block(r0, qa, o_s[:QROWS])
        finish_block(r1, qb, o_s[QROWS:])

    n_buckets = 8
    pairs = N_SLC // 2 // n_buckets
    for kb in range(n_buckets):
        nkeys = (kb + 1) * pairs * 2 * SLC_BLOCK

        def body(ip, carry, nkeys=nkeys):
            qpair(ip, nkeys)
            return carry

        lax.fori_loop(kb * pairs, (kb + 1) * pairs, body, 0)


def _nsa(q, ks, kw, vs, vw, kc2, vc2, gt, wak, wbk, w2k, wav, wbv, w2v, p1k, p2k, p1v, p2v, rrep, slopes, ovl):
    b = q.shape[0]
    hs = lambda w: pl.BlockSpec((1, SEQ, w), lambda h, i: (i, 0, h))
    perh = lambda s: pl.BlockSpec((1,) + s, lambda h, i: (h,) + (0,) * len(s))
    cblk = pl.BlockSpec((1, N_CMP_PAD, CMP_STRIDE * 2 * HEAD_DIM), lambda h, i: (i, 0, 0))
    w1 = (CMP_STRIDE * 2 * HEAD_DIM, CMP_HIDDEN)
    return pl.pallas_call(
        _nsa_body,
        grid=(NSA_KV_HEADS, b),
        in_specs=[hs(NSA_GROUP * LANE), hs(LANE), hs(LANE), hs(LANE), hs(LANE), cblk, cblk,
                  pl.BlockSpec((1, SEQ, LANE), lambda h, i: (i, 0, 0)),
                  perh(w1), perh(w1), _const_spec((CMP_HIDDEN, LANE)),
                  perh(w1), perh(w1), _const_spec((CMP_HIDDEN, LANE)),
                  _const_spec((1, w1[0])), _const_spec((1, w1[0])), _const_spec((1, w1[0])), _const_spec((1, w1[0])),
                  perh((LANE, 6 * LANE)), perh((8, LANE)), _const_spec((N_SLC, LANE))],
        out_specs=pl.BlockSpec((1, SEQ, 2 * LANE), lambda h, i: (i, 0, h)),
        out_shape=jax.ShapeDtypeStruct((b, SEQ, NSA_HEADS * HEAD_DIM), BF16),
        scratch_shapes=[pltpu.VMEM((2, SEQ, LANE), F32), pltpu.VMEM((NSA_GROUP, SEQ, LANE), BF16),
                        pltpu.VMEM((SEQ + WIN_PAD, LANE), BF16), pltpu.VMEM((SEQ + WIN_PAD, LANE), BF16),
                        pltpu.VMEM((NSA_GROUP, SEQ, LANE), F32), pltpu.VMEM((QROWS, WIN_KEYS), F32),
                        pltpu.VMEM((2 * QROWS, LANE), F32)],
        compiler_params=pltpu.CompilerParams(dimension_semantics=("arbitrary", "arbitrary"),
                                             vmem_limit_bytes=VMEM_LIMIT),
        name="nsa",
    )(q, ks, kw, vs, vw, kc2, vc2, gt, wak, wbk, w2k, wav, wbv, w2v, p1k, p2k, p1v, p2v, rrep, slopes, ovl)


def _gla_body(q_ref, k_ref, v_ref, r_ref, la_ref, ng_ref, o_ref, oi_ref, qd_ref, inc_ref, dec_ref, st_ref):
    n_chunks = SEQ // GLA_CHUNK
    per_group = GLA_GROUP // GLA_CHUNK
    ri = lax.broadcasted_iota(jnp.int32, (GLA_GROUP, GLA_GROUP), 0)
    ci = lax.broadcasted_iota(jnp.int32, (GLA_GROUP, GLA_GROUP), 1)
    same_chunk = (ri >> 6) == (ci >> 6)
    tri = same_chunk & (ci <= ri)
    t_mat = jnp.where(tri, 1.0, 0.0).astype(BF16)
    o_mat = jnp.where(same_chunk, 1.0, 0.0).astype(BF16)
    head_g = lax.broadcasted_iota(jnp.int32, (GLA_GROUP, GLA_KEY_WIDTH), 1) >> 6
    head_c = lax.broadcasted_iota(jnp.int32, (GLA_CHUNK, GLA_KEY_WIDTH), 1) >> 6
    head_s = lax.broadcasted_iota(jnp.int32, (GLA_DV, GLA_KEY_WIDTH), 1) >> 6

    def group(gi, carry):
        r0 = pl.multiple_of(gi * GLA_GROUP, GLA_GROUP)
        rows = pl.ds(r0, GLA_GROUP)
        hi, lo = _split_bf16(la_ref[0, rows, :])
        b = _dot(t_mat, hi) + _dot(t_mat, lo)
        b_last = _dot(o_mat, hi) + _dot(o_mat, lo)
        k = k_ref[0, rows, :].astype(F32)
        q_d = q_ref[0, rows, :].astype(F32) * jnp.exp(b)
        k_d = (k * jnp.exp(-b)).astype(BF16)
        k_s = (k * jnp.exp(b_last - b)).astype(BF16)
        qd_ref[rows, :] = q_d.astype(BF16)
        v = v_ref[0, rows, :]
        for h in range(GLA_HEADS):
            qm = jnp.where(head_g == h, q_d, 0.0).astype(BF16)
            att = jnp.where(tri, _dot_nt(qm, k_d), 0.0)
            oi_ref[rows, h * GLA_DV:(h + 1) * GLA_DV] = _dot(att.astype(BF16), v[:, h * GLA_DV:(h + 1) * GLA_DV])
        decay = jnp.exp(b_last)
        for c in range(per_group):
            cr = slice(c * GLA_CHUNK, (c + 1) * GLA_CHUNK)
            inc_full = _dot_tn(v[cr, :], k_s[cr, :])
            inc = jnp.zeros((GLA_DV, GLA_KEY_WIDTH), F32)
            for h in range(GLA_HEADS):
                inc = inc + jnp.where(head_s == h, inc_full[h * GLA_DV:(h + 1) * GLA_DV, :], 0.0)
            inc_ref[gi * per_group + c] = inc
            dec_ref[gi * per_group + c] = decay[c * GLA_CHUNK:c * GLA_CHUNK + 8, :]
        return carry

    lax.fori_loop(0, SEQ // GLA_GROUP, group, 0)

    state = jnp.zeros((GLA_DV, GLA_KEY_WIDTH), F32)
    for n in range(n_chunks):
        st_ref[n] = state.astype(BF16)
        state = state * dec_ref[n, 0:1, :] + inc_ref[n]

    def emit(gi, carry):
        r0 = pl.multiple_of(gi * GLA_GROUP, GLA_GROUP)
        for c in range(per_group):
            rows = pl.ds(r0 + c * GLA_CHUNK, GLA_CHUNK)
            q_d = qd_ref[rows, :]
            st = st_ref[gi * per_group + c]
            for h in range(GLA_HEADS):
                cols = slice(h * GLA_DV, (h + 1) * GLA_DV)
                qm = jnp.where(head_c == h, q_d, jnp.zeros_like(q_d))
                o = oi_ref[rows, cols] + _dot_nt(qm, st)
                o = o * lax.rsqrt(jnp.mean(o * o, axis=-1, keepdims=True) + RMS_EPS) * ng_ref[...]
                r = r_ref[0, rows, cols].astype(F32)
                o_ref[0, rows, cols] = (o * (r * jax.nn.sigmoid(r))).astype(BF16)
        return carry

    lax.fori_loop(0, SEQ // GLA_GROUP, emit, 0)


def _gla(qb, kb, vb, rb, la, ng):
    b = qb.shape[0]
    n_chunks = SEQ // GLA_CHUNK
    blk = lambda w: pl.BlockSpec((1, SEQ, w), lambda i: (i, 0, 0))
    return pl.pallas_call(
        _gla_body,
        grid=(b,),
        in_specs=[blk(GLA_KEY_WIDTH), blk(GLA_KEY_WIDTH), blk(GLA_VAL_WIDTH), blk(GLA_VAL_WIDTH),
                  blk(GLA_KEY_WIDTH), _const_spec((1, GLA_DV))],
        out_specs=blk(GLA_VAL_WIDTH),
        out_shape=jax.ShapeDtypeStruct((b, SEQ, GLA_VAL_WIDTH), BF16),
        scratch_shapes=[pltpu.VMEM((SEQ, GLA_VAL_WIDTH), F32), pltpu.VMEM((SEQ, GLA_KEY_WIDTH), BF16),
                        pltpu.VMEM((n_chunks, GLA_DV, GLA_KEY_WIDTH), F32),
                        pltpu.VMEM((n_chunks, 8, GLA_KEY_WIDTH), F32),
                        pltpu.VMEM((n_chunks, GLA_DV, GLA_KEY_WIDTH), BF16)],
        compiler_params=pltpu.CompilerParams(dimension_semantics=("arbitrary",), vmem_limit_bytes=VMEM_LIMIT),
        name="gla",
    )(qb, kb, vb, rb, la, ng)


def _mixx_body(x_ref, on_ref, og_ref, gm_ref, wn_ref, wg_ref, wo_ref, lx_ref, wq_ref, km_ref, vm_ref, wxo_ref,
               o_ref):
    gm = gm_ref[...].astype(F32)
    mix = (gm[:, :D_MODEL] * _dot(on_ref[...], wn_ref[...])
           + gm[:, D_MODEL:] * _dot(og_ref[...], wg_ref[...]))
    x1 = x_ref[...] + _dot(mix.astype(BF16), wo_ref[...])
    hx = _rms(x1, lx_ref[...]).astype(BF16)
    q = (_dot(hx, wq_ref[...]) * (XATTN_HEAD_DIM ** -0.5)).astype(BF16)
    outs = []
    for h in range(XATTN_HEADS):
        sl = slice(h * XATTN_HEAD_DIM, (h + 1) * XATTN_HEAD_DIM)
        s = _dot_nt(q[:, sl], km_ref[0, :, sl])
        e = jnp.exp(s - jnp.max(s, axis=-1, keepdims=True))
        l = jnp.sum(e, axis=-1, keepdims=True)
        outs.append((_dot(e.astype(BF16), vm_ref[0, :, sl]) / l).astype(BF16))
    o = jnp.concatenate(outs, axis=-1)
    o_ref[...] = x1 + _dot(o, wxo_ref[...])


def _mixx(x2, o_nsa, o_gla, gm, wn, wg, wo, lx, wq, km, vm, wxo):
    n = x2.shape[0]
    tiles_per_seq = SEQ // TM
    tile = lambda w: pl.BlockSpec((TM, w), lambda i: (i, 0))
    mblk = pl.BlockSpec((1, MEM_LEN, XATTN_WIDTH), lambda i: (i // tiles_per_seq, 0, 0))
    return pl.pallas_call(
        _mixx_body,
        grid=(n // TM,),
        in_specs=[tile(D_MODEL), tile(XATTN_WIDTH), tile(GLA_VAL_WIDTH), tile(2 * D_MODEL),
                  _const_spec((XATTN_WIDTH, D_MODEL)), _const_spec((GLA_VAL_WIDTH, D_MODEL)),
                  _const_spec((D_MODEL, D_MODEL)), _const_spec((1, D_MODEL)),
                  _const_spec((D_MODEL, XATTN_WIDTH)), mblk, mblk, _const_spec((XATTN_WIDTH, D_MODEL))],
        out_specs=tile(D_MODEL),
        out_shape=jax.ShapeDtypeStruct((n, D_MODEL), F32),
        compiler_params=pltpu.CompilerParams(dimension_semantics=("arbitrary",), vmem_limit_bytes=VMEM_LIMIT),
        name="mixx",
    )(x2, o_nsa, o_gla, gm, wn, wg, wo, lx, wq, km, vm, wxo)


def _ffn_body(x_ref, lf_ref, wu_ref, wgt_ref, cw_ref, cb_ref, wd_ref, lfin_ref, o_ref, ubuf_ref, carry_ref):
    @pl.when(pl.program_id(0) % (SEQ // TM) == 0)
    def _():
        carry_ref[...] = jnp.zeros_like(carry_ref)

    x = x_ref[...]
    hf = _rms(x, lf_ref[...]).astype(BF16)
    acc = jnp.zeros((TM, D_MODEL), F32)
    for c in range(FFN_DIM // FFN_CHUNK):
        cols = slice(c * FFN_CHUNK, (c + 1) * FFN_CHUNK)
        ubuf_ref[0:8, :] = carry_ref[:, cols]
        ubuf_ref[8:, :] = _dot(hf, wu_ref[:, cols])
        carry_ref[:, cols] = ubuf_ref[TM:TM + 8, :]
        u = (cw_ref[0:1, cols] * ubuf_ref[6:6 + TM, :] + cw_ref[1:2, cols] * ubuf_ref[7:7 + TM, :]
             + cw_ref[2:3, cols] * ubuf_ref[8:8 + TM, :] + cb_ref[:, cols])
        act = (jax.nn.gelu(u) * _dot(hf, wgt_ref[:, cols])).astype(BF16)
        acc = acc + _dot(act, wd_ref[cols, :])
    o_ref[...] = _rms(x + acc, lfin_ref[...])


def _ffn(x2, lf, wu, wgt, cw, cb, wd, lfin):
    n = x2.shape[0]
    tile = pl.BlockSpec((TM, D_MODEL), lambda i: (i, 0))
    return pl.pallas_call(
        _ffn_body,
        grid=(n // TM,),
        in_specs=[tile, _const_spec((1, D_MODEL)), _const_spec((D_MODEL, FFN_DIM)), _const_spec((D_MODEL, FFN_DIM)),
                  _const_spec((CONV_WIDTH, FFN_DIM)), _const_spec((1, FFN_DIM)), _const_spec((FFN_DIM, D_MODEL)),
                  _const_spec((1, D_MODEL))],
        out_specs=tile,
        out_shape=jax.ShapeDtypeStruct((n, D_MODEL), F32),
        scratch_shapes=[pltpu.VMEM((TM + 8, FFN_CHUNK), F32), pltpu.VMEM((8, FFN_DIM), F32)],
        compiler_params=pltpu.CompilerParams(dimension_semantics=("arbitrary",), vmem_limit_bytes=VMEM_LIMIT),
        name="ffn",
    )(x2, lf, wu, wgt, cw, cb, wd, lfin)


def _layout_inproj(w_in):
    o = 0
    seg = {}
    for name, wdt in (("q", 512), ("kc", 128), ("vc", 128), ("ks", 128), ("vs", 128), ("kw", 128), ("vw", 128),
                      ("gt", 24), ("qb", 256), ("kb", 256), ("vb", 512), ("rb", 512), ("al", 16), ("mg", 2048)):
        seg[name] = w_in[:, o:o + wdt]
        o += wdt
    z64 = jnp.zeros((D_MODEL, HEAD_DIM), F32)
    head = lambda w, h: w[:, h * HEAD_DIM:(h + 1) * HEAD_DIM]
    q_scale = (HEAD_DIM ** -0.5) * LOG2E
    cols = [jnp.concatenate([head(seg["q"], hh) * q_scale, z64], 1) for hh in range(NSA_HEADS)]
    cols += [jnp.concatenate([head(seg["ks"], h), z64], 1) for h in range(NSA_KV_HEADS)]
    cols += [jnp.concatenate([head(seg["kw"], h), z64], 1) for h in range(NSA_KV_HEADS)]
    cols += [jnp.concatenate([head(seg["vs"], h)] * 2, 1) for h in range(NSA_KV_HEADS)]
    cols += [jnp.concatenate([head(seg["vw"], h)] * 2, 1) for h in range(NSA_KV_HEADS)]
    cols += [seg["kc"], seg["vc"]]
    cols += [jnp.pad(seg["gt"], ((0, 0), (0, LANE - 3 * NSA_HEADS)))]
    cols += [seg["qb"] * (GLA_DK ** -0.5), seg["kb"], seg["vb"], seg["rb"]]
    cols += [jnp.pad(seg["al"], ((0, 0), (0, LANE - GLA_RANK)))]
    cols += [seg["mg"]]
    return jnp.concatenate(cols, axis=1).astype(BF16)


def _position_constants():
    slopes = 2.0 ** (-np.arange(1, NSA_HEADS + 1, dtype=np.float64)) * LOG2E
    qconst = np.zeros((1, NSA_HEADS * LANE), np.float32)
    for hh in range(NSA_HEADS):
        qconst[0, hh * LANE + AUG_HI] = slopes[hh]
        qconst[0, hh * LANE + AUG_LO] = slopes[hh]
    t = np.arange(SEQ)
    kaug_s = np.zeros((SEQ, 2 * LANE), np.float32)
    kaug_w = np.zeros((SEQ, 2 * LANE), np.float32)
    for h in range(NSA_KV_HEADS):
        kaug_s[t, h * LANE + AUG_SEL + t // SLC_BLOCK] = 1.0
        for a in (kaug_s, kaug_w):
            a[:, h * LANE + AUG_HI] = (t // SLC_BLOCK) * SLC_BLOCK
            a[:, h * LANE + AUG_LO] = t % SLC_BLOCK
    slope_rows = np.zeros((NSA_KV_HEADS, 8, LANE), np.float32)
    rrep = np.zeros((NSA_KV_HEADS, LANE, 6 * LANE), np.float32)
    for h in range(NSA_KV_HEADS):
        for g in range(NSA_GROUP):
            slope_rows[h, g, :] = slopes[h * NSA_GROUP + g]
            for c in range(3):
                pp, half = g // 2, g % 2
                c0 = (c * 2 + pp) * LANE + half * HEAD_DIM
                rrep[h, (h * NSA_GROUP + g) * 3 + c, c0:c0 + HEAD_DIM] = 1.0
    ovl = np.zeros((N_SLC, N_CMP_PAD), np.float32)
    for n in range(N_CMP_PAD - 1):
        for tok in range(n * CMP_STRIDE, n * CMP_STRIDE + CMP_BLOCK):
            ovl[tok // SLC_BLOCK, n] += 1.0 / CMP_BLOCK
    return qconst, kaug_s, kaug_w, slope_rows, rrep, ovl


def _layout_compress(w1, w2, pos, dup):
    w1 = w1.reshape(2, CMP_STRIDE, 1, HEAD_DIM, CMP_HIDDEN)
    sel = jnp.eye(NSA_KV_HEADS, dtype=F32)[:, None, None, :, None, None]
    wh = (w1[None] * sel).reshape(NSA_KV_HEADS, 2, CMP_STRIDE * NSA_KV_HEADS * HEAD_DIM, CMP_HIDDEN)
    p = jnp.broadcast_to(pos.reshape(2, CMP_STRIDE, 1, HEAD_DIM), (2, CMP_STRIDE, NSA_KV_HEADS, HEAD_DIM))
    p = p.reshape(2, 1, CMP_STRIDE * NSA_KV_HEADS * HEAD_DIM)
    w2p = jnp.concatenate([w2, w2 if dup else jnp.zeros_like(w2)], axis=1)
    return wh[:, 0].astype(BF16), wh[:, 1].astype(BF16), w2p.astype(BF16), p[0], p[1]


def kernel(x, mem, ln_mix_g, w_in, nsa_gate_b, cmp_pos_k, cmp_w1_k, cmp_w2_k, cmp_pos_v, cmp_w1_v, cmp_w2_v,
           gla_w_alpha2, gla_b_alpha, gla_norm_g, w_branch_nsa, w_branch_gla, w_out, ln_x_g, ln_mem_g, w_xq,
           w_xkv, w_xo, ln_ffn_g, w_up, conv_w, conv_b, w_down, ln_final_g):
    b = x.shape[0]
    n = b * SEQ
    row = lambda v: v.reshape(1, -1).astype(F32)
    qconst, kaug_s, kaug_w, slope_rows, rrep, ovl = _position_constants()

    x2 = x.reshape(n, D_MODEL)
    gate_b = jnp.pad(nsa_gate_b[0], (0, LANE - 3 * NSA_HEADS)).reshape(1, LANE)
    wa2 = jnp.pad(gla_w_alpha2[0], ((0, LANE - GLA_RANK), (0, 0)))
    (q, ks, kw, vs, vw, kc, vc, gt, qb, kb, vb, rb, la, gm) = _inproj(
        x2, row(ln_mix_g[0]), _layout_inproj(w_in[0]), qconst, kaug_s, kaug_w, gate_b, wa2, row(gla_b_alpha[0]))

    s3 = lambda a: a.reshape(b, SEQ, a.shape[-1])
    chunked = lambda a: a.reshape(b, N_CMP_PAD, CMP_STRIDE * NSA_KV_HEADS * HEAD_DIM)
    wak, wbk, w2k, p1k, p2k = _layout_compress(cmp_w1_k[0], cmp_w2_k[0], cmp_pos_k[0], dup=False)
    wav, wbv, w2v, p1v, p2v = _layout_compress(cmp_w1_v[0], cmp_w2_v[0], cmp_pos_v[0], dup=True)
    o_nsa = _nsa(s3(q), s3(ks), s3(kw), s3(vs), s3(vw), chunked(kc), chunked(vc), s3(gt),
                 wak, wbk, w2k, wav, wbv, w2v, p1k, p2k, p1v, p2v,
                 jnp.asarray(rrep, BF16), jnp.asarray(slope_rows), jnp.asarray(ovl, BF16))

    o_gla = _gla(s3(qb), s3(kb), s3(vb), s3(rb), s3(la), row(gla_norm_g[0]))

    km, vm = _memkv(mem, row(ln_mem_g[0]), w_xkv[0].astype(BF16))
    x_mid = _mixx(x2, o_nsa.reshape(n, -1), o_gla.reshape(n, -1), gm,
                  w_branch_nsa[0].astype(BF16), w_branch_gla[0].astype(BF16), w_out[0].astype(BF16),
                  row(ln_x_g[0]), w_xq[0].astype(BF16), km, vm, w_xo[0].astype(BF16))

    out = _ffn(x_mid, row(ln_ffn_g[0]), w_up[0][:, :FFN_DIM].astype(BF16), w_up[0][:, FFN_DIM:].astype(BF16),
               conv_w[0], conv_b[0].reshape(1, FFN_DIM), w_down[0].astype(BF16), row(ln_final_g))
    return out.reshape(b, SEQ, D_MODEL)
```

```python
import math

import jax
import jax.numpy as jnp
import numpy as np
from jax import lax
from jax.experimental import pallas as pl
from jax.experimental.pallas import tpu as pltpu

F32 = jnp.float32
BF16 = jnp.bfloat16

D_MODEL = 1024
SEQ = 2048
MEM_LEN = 256
HEAD_DIM = 64
NSA_HEADS = 8
NSA_KV_HEADS = 2
NSA_GROUP = 4
CMP_BLOCK = 32
CMP_STRIDE = 16
CMP_HIDDEN = 128
N_CMP_PAD = SEQ // CMP_STRIDE
SLC_BLOCK = 64
N_SLC = SEQ // SLC_BLOCK
N_SELECT = 8
WINDOW = 256
GLA_HEADS = 4
GLA_DK = 64
GLA_DV = 128
GLA_KEY_WIDTH = 256
GLA_VAL_WIDTH = 512
GLA_RANK = 16
GLA_TAU = 16.0
GLA_CHUNK = 64
GLA_GROUP = 256
XATTN_HEADS = 4
XATTN_HEAD_DIM = 128
XATTN_WIDTH = 512
FFN_DIM = 2816
CONV_WIDTH = 3
RMS_EPS = 1e-6
NEG_INF = -1e30
LOG2E = math.log2(math.e)

LANE = 128
VMEM_LIMIT = 56 * 1024 * 1024

AUG_SEL = HEAD_DIM
AUG_HI = HEAD_DIM + N_SLC
AUG_LO = AUG_HI + 1

C_Q = 0
C_KS = C_Q + NSA_HEADS * HEAD_DIM
C_KW = C_KS + LANE
C_VS = C_KW + LANE
C_VW = C_VS + LANE
C_KC = C_VW + LANE
C_VC = C_KC + LANE
C_GT = C_VC + LANE
C_AL = C_GT + LANE
C_QB = C_AL + LANE
C_KB = C_QB + GLA_KEY_WIDTH
C_VB = C_KB + GLA_KEY_WIDTH
C_RB = C_VB + GLA_VAL_WIDTH
C_MG = C_RB + GLA_VAL_WIDTH
C_END = C_MG + 2 * D_MODEL

TM = 512
QUAD_TOKENS = 4 * SLC_BLOCK
QUAD_ROWS = NSA_GROUP * QUAD_TOKENS
WIN_PAD = WINDOW
WIN_KEYS = WIN_PAD + QUAD_TOKENS
KEY_CHUNK = 512
FFN_CHUNK = 1408


def _rms(x, g):
    return x * lax.rsqrt(jnp.mean(x * x, axis=-1, keepdims=True) + RMS_EPS) * g


def _dot(a, b, **kw):
    return jnp.dot(a, b, preferred_element_type=F32, **kw)


def _dot_nt(a, b):
    return lax.dot_general(a, b, (((1,), (1,)), ((), ())), preferred_element_type=F32)


def _dot_tn(a, b):
    return lax.dot_general(a, b, (((0,), (0,)), ((), ())), preferred_element_type=F32)


def _split_bf16(x):
    hi = x.astype(BF16)
    return hi, (x - hi.astype(F32)).astype(BF16)


def _const_spec(shape):
    return pl.BlockSpec(shape, lambda *_: (0,) * len(shape))


def _inproj_body(x_ref, g_ref, w_ref, qc_ref, kas_ref, kaw_ref, gb_ref, wa2_ref, ba_ref,
                 q_o, ks_o, kw_o, vs_o, vw_o, kc_o, vc_o, gt_o, qb_o, kb_o, vb_o, rb_o, la_o, gm_o):
    h = _rms(x_ref[...], g_ref[...]).astype(BF16)

    def proj(lo, hi):
        return _dot(h, w_ref[:, lo:hi])

    low = lax.broadcasted_iota(jnp.int32, (TM, LANE), 1) < HEAD_DIM

    def head_slots(x, fill):
        swapped = pltpu.roll(x, HEAD_DIM, axis=1)
        return [jnp.where(low, x, fill(0, x, swapped)), jnp.where(low, swapped, fill(1, swapped, x))]

    def group(wide, c0, base):
        return wide[:, c0 - base:c0 - base + LANE]

    pq = proj(C_Q, C_KS)
    for pr in range(NSA_HEADS // 2):
        slots = head_slots(group(pq, C_Q + pr * LANE, C_Q),
                           lambda i, mine, other: qc_ref[:, (2 * pr + i) * LANE:(2 * pr + i + 1) * LANE])
        for i, sl in enumerate(slots):
            q_o[:, (2 * pr + i) * LANE:(2 * pr + i + 1) * LANE] = sl.astype(BF16)
    pkv = proj(C_KS, C_KC)
    for c0, aug_ref, out in ((C_KS, kas_ref, ks_o), (C_KW, kaw_ref, kw_o)):
        slots = head_slots(group(pkv, c0, C_KS), lambda i, mine, other: aug_ref[:, i * LANE:(i + 1) * LANE])
        for i, sl in enumerate(slots):
            out[:, i * LANE:(i + 1) * LANE] = sl.astype(BF16)
    for c0, out in ((C_VS, vs_o), (C_VW, vw_o)):
        for i, sl in enumerate(head_slots(group(pkv, c0, C_KS), lambda i, mine, other: other)):
            out[:, i * LANE:(i + 1) * LANE] = sl.astype(BF16)
    pmisc = proj(C_KC, C_QB)
    kc_o[...] = group(pmisc, C_KC, C_KC).astype(BF16)
    vc_o[...] = group(pmisc, C_VC, C_KC).astype(BF16)
    gt_o[...] = jax.nn.sigmoid(group(pmisc, C_GT, C_KC) + gb_ref[...]).astype(BF16)
    qb_o[...] = proj(C_QB, C_KB).astype(BF16)
    kb_o[...] = proj(C_KB, C_VB).astype(BF16)
    vb_o[...] = proj(C_VB, C_RB).astype(BF16)
    rb_o[...] = proj(C_RB, C_MG).astype(BF16)
    z = _dot(group(pmisc, C_AL, C_KC), wa2_ref[...], precision=lax.Precision.HIGHEST) + ba_ref[...]
    log_sig = jnp.minimum(z, 0.0) - jnp.log1p(jnp.exp(-jnp.abs(z)))
    la_o[...] = log_sig * (1.0 / GLA_TAU)
    gm_o[...] = jax.nn.sigmoid(proj(C_MG, C_END)).astype(BF16)


def _inproj(x2, g, wcat, qconst, kaug_s, kaug_w, gate_b, wa2, ba):
    n = x2.shape[0]
    tiles_per_seq = SEQ // TM
    widths = [(NSA_HEADS * LANE, BF16), (LANE * 2, BF16), (LANE * 2, BF16), (LANE * 2, BF16), (LANE * 2, BF16),
              (LANE, BF16), (LANE, BF16), (LANE, BF16), (GLA_KEY_WIDTH, BF16), (GLA_KEY_WIDTH, BF16),
              (GLA_VAL_WIDTH, BF16), (GLA_VAL_WIDTH, BF16), (GLA_KEY_WIDTH, F32), (2 * D_MODEL, BF16)]
    tile = lambda w: pl.BlockSpec((TM, w), lambda i: (i, 0))
    return pl.pallas_call(
        _inproj_body,
        grid=(n // TM,),
        in_specs=[tile(D_MODEL), _const_spec((1, D_MODEL)), _const_spec((D_MODEL, C_END)),
                  _const_spec((1, NSA_HEADS * LANE)),
                  pl.BlockSpec((TM, 2 * LANE), lambda i: (i % tiles_per_seq, 0)),
                  pl.BlockSpec((TM, 2 * LANE), lambda i: (i % tiles_per_seq, 0)),
                  _const_spec((1, LANE)), _const_spec((LANE, GLA_KEY_WIDTH)), _const_spec((1, GLA_KEY_WIDTH))],
        out_specs=[tile(w) for w, _ in widths],
        out_shape=[jax.ShapeDtypeStruct((n, w), dt) for w, dt in widths],
        compiler_params=pltpu.CompilerParams(dimension_semantics=("arbitrary",), vmem_limit_bytes=VMEM_LIMIT),
        name="inproj",
    )(x2, g, wcat, qconst, kaug_s, kaug_w, gate_b, wa2, ba)


def _memkv_body(m_ref, g_ref, w_ref, k_o, v_o):
    hm = _rms(m_ref[0], g_ref[...]).astype(BF16)
    kv = _dot(hm, w_ref[...])
    k_o[0] = kv[:, :XATTN_WIDTH].astype(BF16)
    v_o[0] = kv[:, XATTN_WIDTH:].astype(BF16)


def _memkv(mem, g, w):
    b = mem.shape[0]
    blk = pl.BlockSpec((1, MEM_LEN, XATTN_WIDTH), lambda i: (i, 0, 0))
    return pl.pallas_call(
        _memkv_body,
        grid=(b,),
        in_specs=[pl.BlockSpec((1, MEM_LEN, D_MODEL), lambda i: (i, 0, 0)), _const_spec((1, D_MODEL)),
                  _const_spec((D_MODEL, 2 * XATTN_WIDTH))],
        out_specs=[blk, blk],
        out_shape=[jax.ShapeDtypeStruct((b, MEM_LEN, XATTN_WIDTH), BF16)] * 2,
        compiler_params=pltpu.CompilerParams(dimension_semantics=("arbitrary",)),
        name="memkv",
    )(mem, g, w)


def _compress(c_ref, wa_ref, wb_ref, w2_ref, p1_ref, p2_ref):
    c = c_ref[0].astype(F32)
    a = _dot((c + p1_ref[...]).astype(BF16), wa_ref[0])
    b = _dot((c + p2_ref[...]).astype(BF16), wb_ref[0])
    hid = a + pltpu.roll(b, N_CMP_PAD - 1, axis=0)
    return _dot(jax.nn.gelu(hid).astype(BF16), w2_ref[...]).astype(BF16)


def _nsa_constants(slope_ref, biasc_ref, wbias_ref, dbias_ref):
    t_col = lax.broadcasted_iota(jnp.int32, (SEQ, LANE), 0)
    n_row = lax.broadcasted_iota(jnp.int32, (SEQ, LANE), 1)
    dist_c = (t_col - (n_row * CMP_STRIDE + (CMP_BLOCK - 1))).astype(F32)
    for g in range(NSA_GROUP):
        biasc_ref[g] = jnp.where(dist_c >= 0.0, -slope_ref[0, g:g + 1, :] * dist_c, NEG_INF)
    col = lax.broadcasted_iota(jnp.int32, (QUAD_ROWS, WIN_KEYS), 1)
    row = lax.broadcasted_iota(jnp.int32, (QUAD_ROWS, WIN_KEYS), 0)
    dist_w = ((row >> 8) << 6) + (row & (SLC_BLOCK - 1)) + WIN_PAD - col
    wbias_ref[...] = jnp.where((dist_w >= 0) & (dist_w < WINDOW), 0.0, NEG_INF)
    col = lax.broadcasted_iota(jnp.int32, (QUAD_ROWS, QUAD_TOKENS), 1)
    row = lax.broadcasted_iota(jnp.int32, (QUAD_ROWS, QUAD_TOKENS), 0)
    blk_r, blk_c = row >> 8, col >> 6
    causal = (col & (SLC_BLOCK - 1)) <= (row & (SLC_BLOCK - 1))
    dbias_ref[...] = jnp.where((blk_c < blk_r) | ((blk_c == blk_r) & causal), 0.0, NEG_INF)


def _nsa_body(q_ref, ks_ref, kw_ref, vs_ref, vw_ref, kc_ref, vc_ref, gt_ref,
              wak_ref, wbk_ref, w2k_ref, wav_ref, wbv_ref, w2v_ref, p1k_ref, p2k_ref, p1v_ref, p2v_ref,
              rrep_ref, slope_ref, ovl_ref, o_ref,
              ocmp_ref, qs_ref, kwp_ref, vwp_ref, biasc_ref, wbias_ref, dbias_ref, s_ref, e_ref, sw_ref, ew_ref):
    @pl.when(pl.program_id(1) == 0)
    def _():
        _nsa_constants(slope_ref, biasc_ref, wbias_ref, dbias_ref)

    kcmp = _compress(kc_ref, wak_ref, wbk_ref, w2k_ref, p1k_ref, p2k_ref)
    vcmp = _compress(vc_ref, wav_ref, wbv_ref, w2v_ref, p1v_ref, p2v_ref)

    low_half = lax.broadcasted_iota(jnp.int32, (SEQ, LANE), 1) < HEAD_DIM
    has_key = lax.broadcasted_iota(jnp.int32, (SEQ, 1), 0) >= CMP_BLOCK - 1
    imp = jnp.zeros((SEQ, LANE), F32)
    o_even = None
    for g in range(NSA_GROUP):
        s = _dot_nt(q_ref[0, :, g * LANE:(g + 1) * LANE], kcmp) + biasc_ref[g]
        e = jnp.exp2(s - jnp.max(s, axis=-1, keepdims=True))
        l = jnp.sum(e, axis=-1, keepdims=True)
        p = e * jnp.where(has_key, 1.0 / l, 0.0)
        imp = imp + p
        oc = _dot(p.astype(BF16), vcmp)
        if g % 2 == 0:
            o_even = oc
        else:
            ocmp_ref[g // 2] = jnp.where(low_half, o_even, oc)

    hi, lo = _split_bf16(imp.T)
    ovl = ovl_ref[...]
    imp_b = _dot(ovl, hi) + _dot(ovl, lo)
    j = lax.broadcasted_iota(jnp.int32, (N_SLC, SEQ), 0)
    cur = lax.broadcasted_iota(jnp.int32, (N_SLC, SEQ), 1) >> 6
    forced = (j == 0) | (j == cur) | (j == cur - 1)
    val = jnp.where(forced, float(NSA_GROUP + 1), jnp.where(j > cur, -1.0, imp_b))
    j8 = lax.broadcasted_iota(jnp.int32, (8, SEQ), 0)
    vals = [val[8 * r:8 * r + 8, :] for r in range(N_SLC // 8)]
    ranks = [jnp.zeros((8, SEQ), F32) for _ in vals]
    for jp in range(N_SLC):
        vj = val[jp:jp + 1, :]
        for r, vr in enumerate(vals):
            if 8 * r + 7 <= jp:
                beats = vj > vr
            elif 8 * r > jp:
                beats = vj >= vr
            else:
                beats = (vj > vr) | ((vj == vr) & (j8 > jp - 8 * r))
            ranks[r] = ranks[r] + jnp.where(beats, 1.0, 0.0)
    rank = jnp.concatenate(ranks, axis=0)
    sel_bias = jnp.where((rank < float(N_SELECT)) & (j <= cur), 0.0, NEG_INF)
    aug_t = jnp.concatenate([jnp.zeros((AUG_SEL, SEQ), F32), sel_bias,
                             jnp.zeros((LANE - AUG_SEL - N_SLC, SEQ), F32)], axis=0)
    aug = aug_t.T.astype(BF16)
    for g in range(NSA_GROUP):
        qs_ref[g] = q_ref[0, :, g * LANE:(g + 1) * LANE] + aug

    pad_lane = lax.broadcasted_iota(jnp.int32, (WIN_PAD, LANE), 1)
    kwp_ref[0:WIN_PAD, :] = jnp.where(pad_lane == AUG_HI, NEG_INF, 0.0).astype(BF16)
    vwp_ref[0:WIN_PAD, :] = jnp.zeros((WIN_PAD, LANE), BF16)
    kwp_ref[WIN_PAD:, :] = kw_ref[0]
    vwp_ref[WIN_PAD:, :] = vw_ref[0]

    lane64 = lax.broadcasted_iota(jnp.int32, (SLC_BLOCK, LANE), 1) < HEAD_DIM

    def softmax_pv(s_buf, e_buf, rows, nkeys, v):
        m = jnp.max(s_buf[rows, 0:nkeys], axis=-1, keepdims=True)
        e = jnp.exp2(s_buf[rows, 0:nkeys] - m)
        e_buf[rows, 0:nkeys] = e.astype(BF16)
        return _dot(e_buf[rows, 0:nkeys], v) * (1.0 / jnp.sum(e, axis=-1, keepdims=True))

    def quad(iq):
        r0 = iq * QUAD_TOKENS
        nkeys = r0 + QUAD_TOKENS
        qs = jnp.concatenate([qs_ref[g, r0 + bq * SLC_BLOCK:r0 + (bq + 1) * SLC_BLOCK, :]
                              for bq in range(QUAD_TOKENS // SLC_BLOCK) for g in range(NSA_GROUP)], axis=0)
        halves = [slice(hf * QUAD_ROWS // 2, (hf + 1) * QUAD_ROWS // 2) for hf in range(2)]
        for rows in halves:
            for c0 in range(0, r0, KEY_CHUNK):
                c1 = min(c0 + KEY_CHUNK, r0)
                s_ref[rows, c0:c1] = _dot_nt(qs[rows], ks_ref[0, c0:c1, :])
            s_ref[rows, r0:nkeys] = _dot_nt(qs[rows], ks_ref[0, r0:nkeys, :]) + dbias_ref[rows, :]
        sw_ref[...] = _dot_nt(qs, kwp_ref[r0:r0 + WIN_KEYS, :]) + wbias_ref[...]
        o_s = jnp.concatenate([softmax_pv(s_ref, e_ref, rows, nkeys, vs_ref[0, 0:nkeys, :]) for rows in halves],
                              axis=0)
        o_w = softmax_pv(sw_ref, ew_ref, slice(None), WIN_KEYS, vwp_ref[r0:r0 + WIN_KEYS, :])
        gt = _dot(gt_ref[0, r0:r0 + QUAD_TOKENS, :], rrep_ref[0])
        for bq in range(QUAD_TOKENS // SLC_BLOCK):
            tok = slice(r0 + bq * SLC_BLOCK, r0 + (bq + 1) * SLC_BLOCK)
            gtb = gt[bq * SLC_BLOCK:(bq + 1) * SLC_BLOCK]
            for pp in range(2):
                a0 = (bq * NSA_GROUP + 2 * pp) * SLC_BLOCK
                a1, a2 = a0 + SLC_BLOCK, a0 + 2 * SLC_BLOCK
                osl = jnp.where(lane64, o_s[a0:a1], o_s[a1:a2])
                owi = jnp.where(lane64, o_w[a0:a1], o_w[a1:a2])
                out = (gtb[:, pp * LANE:(pp + 1) * LANE] * ocmp_ref[pp, tok, :]
                       + gtb[:, (2 + pp) * LANE:(3 + pp) * LANE] * osl
                       + gtb[:, (4 + pp) * LANE:(5 + pp) * LANE] * owi)
                o_ref[0, tok, pp * LANE:(pp + 1) * LANE] = out.astype(BF16)

    one = jnp.minimum(pl.program_id(1) + 1, 1)
    for iq in range(SEQ // QUAD_TOKENS):
        def body(_, carry, iq=iq):
            quad(iq)
            return carry

        lax.fori_loop(0, one, body, 0)


def _nsa(q, ks, kw, vs, vw, kc2, vc2, gt, wak, wbk, w2k, wav, wbv, w2v, p1k, p2k, p1v, p2v, rrep, slopes, ovl):
    b = q.shape[0]
    hs = lambda w: pl.BlockSpec((1, SEQ, w), lambda h, i: (i, 0, h))
    perh = lambda s: pl.BlockSpec((1,) + s, lambda h, i: (h,) + (0,) * len(s))
    cblk = pl.BlockSpec((1, N_CMP_PAD, CMP_STRIDE * 2 * HEAD_DIM), lambda h, i: (i, 0, 0))
    w1 = (CMP_STRIDE * 2 * HEAD_DIM, CMP_HIDDEN)
    return pl.pallas_call(
        _nsa_body,
        grid=(NSA_KV_HEADS, b),
        in_specs=[hs(NSA_GROUP * LANE), hs(LANE), hs(LANE), hs(LANE), hs(LANE), cblk, cblk,
                  pl.BlockSpec((1, SEQ, LANE), lambda h, i: (i, 0, 0)),
                  perh(w1), perh(w1), _const_spec((CMP_HIDDEN, LANE)),
                  perh(w1), perh(w1), _const_spec((CMP_HIDDEN, LANE)),
                  _const_spec((1, w1[0])), _const_spec((1, w1[0])), _const_spec((1, w1[0])), _const_spec((1, w1[0])),
                  perh((LANE, 6 * LANE)), perh((8, LANE)), _const_spec((N_SLC, LANE))],
        out_specs=pl.BlockSpec((1, SEQ, 2 * LANE), lambda h, i: (i, 0, h)),
        out_shape=jax.ShapeDtypeStruct((b, SEQ, NSA_HEADS * HEAD_DIM), BF16),
        scratch_shapes=[pltpu.VMEM((2, SEQ, LANE), F32), pltpu.VMEM((NSA_GROUP, SEQ, LANE), BF16),
                        pltpu.VMEM((SEQ + WIN_PAD, LANE), BF16), pltpu.VMEM((SEQ + WIN_PAD, LANE), BF16),
                        pltpu.VMEM((NSA_GROUP, SEQ, LANE), F32), pltpu.VMEM((QUAD_ROWS, WIN_KEYS), F32),
                        pltpu.VMEM((QUAD_ROWS, QUAD_TOKENS), F32),
                        pltpu.VMEM((QUAD_ROWS, SEQ), F32), pltpu.VMEM((QUAD_ROWS, SEQ), BF16),
                        pltpu.VMEM((QUAD_ROWS, WIN_KEYS), F32), pltpu.VMEM((QUAD_ROWS, WIN_KEYS), BF16)],
        compiler_params=pltpu.CompilerParams(dimension_semantics=("arbitrary", "arbitrary"),
                                             vmem_limit_bytes=VMEM_LIMIT),
        name="nsa",
    )(q, ks, kw, vs, vw, kc2, vc2, gt, wak, wbk, w2k, wav, wbv, w2v, p1k, p2k, p1v, p2v, rrep, slopes, ovl)


def _gla_body(q_ref, k_ref, v_ref, r_ref, la_ref, ng_ref, o_ref, oi_ref, qd_ref, inc_ref, dec_ref, st_ref):
    n_chunks = SEQ // GLA_CHUNK
    per_group = GLA_GROUP // GLA_CHUNK
    ri = lax.broadcasted_iota(jnp.int32, (GLA_GROUP, GLA_GROUP), 0)
    ci = lax.broadcasted_iota(jnp.int32, (GLA_GROUP, GLA_GROUP), 1)
    same_chunk = (ri >> 6) == (ci >> 6)
    tri = same_chunk & (ci <= ri)
    t_mat = jnp.where(tri, 1.0, 0.0).astype(BF16)
    o_mat = jnp.where(same_chunk, 1.0, 0.0).astype(BF16)
    head_g = lax.broadcasted_iota(jnp.int32, (GLA_GROUP, GLA_KEY_WIDTH), 1) >> 6
    head_c = lax.broadcasted_iota(jnp.int32, (GLA_CHUNK, GLA_KEY_WIDTH), 1) >> 6
    head_s = lax.broadcasted_iota(jnp.int32, (GLA_DV, GLA_KEY_WIDTH), 1) >> 6

    def group(gi, carry):
        r0 = pl.multiple_of(gi * GLA_GROUP, GLA_GROUP)
        rows = pl.ds(r0, GLA_GROUP)
        hi, lo = _split_bf16(la_ref[0, rows, :])
        b = _dot(t_mat, hi) + _dot(t_mat, lo)
        b_last = _dot(o_mat, hi) + _dot(o_mat, lo)
        k = k_ref[0, rows, :].astype(F32)
        q_d = q_ref[0, rows, :].astype(F32) * jnp.exp(b)
        k_d = (k * jnp.exp(-b)).astype(BF16)
        k_s = (k * jnp.exp(b_last - b)).astype(BF16)
        qd_ref[rows, :] = q_d.astype(BF16)
        v = v_ref[0, rows, :]
        for h in range(GLA_HEADS):
            qm = jnp.where(head_g == h, q_d, 0.0).astype(BF16)
            att = jnp.where(tri, _dot_nt(qm, k_d), 0.0)
            oi_ref[rows, h * GLA_DV:(h + 1) * GLA_DV] = _dot(att.astype(BF16), v[:, h * GLA_DV:(h + 1) * GLA_DV])
        decay = jnp.exp(b_last)
        for c in range(per_group):
            cr = slice(c * GLA_CHUNK, (c + 1) * GLA_CHUNK)
            inc_full = _dot_tn(v[cr, :], k_s[cr, :])
            inc = jnp.zeros((GLA_DV, GLA_KEY_WIDTH), F32)
            for h in range(GLA_HEADS):
                inc = inc + jnp.where(head_s == h, inc_full[h * GLA_DV:(h + 1) * GLA_DV, :], 0.0)
            inc_ref[gi * per_group + c] = inc
            dec_ref[gi * per_group + c] = decay[c * GLA_CHUNK:c * GLA_CHUNK + 8, :]
        return carry

    lax.fori_loop(0, SEQ // GLA_GROUP, group, 0)

    state = jnp.zeros((GLA_DV, GLA_KEY_WIDTH), F32)
    for n in range(n_chunks):
        st_ref[n] = state.astype(BF16)
        state = state * dec_ref[n, 0:1, :] + inc_ref[n]

    def emit(gi, carry):
        r0 = pl.multiple_of(gi * GLA_GROUP, GLA_GROUP)
        for c in range(per_group):
            rows = pl.ds(r0 + c * GLA_CHUNK, GLA_CHUNK)
            q_d = qd_ref[rows, :]
            st = st_ref[gi * per_group + c]
            for h in range(GLA_HEADS):
                cols = slice(h * GLA_DV, (h + 1) * GLA_DV)
                qm = jnp.where(head_c == h, q_d, jnp.zeros_like(q_d))
                o = oi_ref[rows, cols] + _dot_nt(qm, st)
                o = o * lax.rsqrt(jnp.mean(o * o, axis=-1, keepdims=True) + RMS_EPS) * ng_ref[...]
                r = r_ref[0, rows, cols].astype(F32)
                o_ref[0, rows, cols] = (o * (r * jax.nn.sigmoid(r))).astype(BF16)
        return carry

    lax.fori_loop(0, SEQ // GLA_GROUP, emit, 0)


def _gla(qb, kb, vb, rb, la, ng):
    b = qb.shape[0]
    n_chunks = SEQ // GLA_CHUNK
    blk = lambda w: pl.BlockSpec((1, SEQ, w), lambda i: (i, 0, 0))
    return pl.pallas_call(
        _gla_body,
        grid=(b,),
        in_specs=[blk(GLA_KEY_WIDTH), blk(GLA_KEY_WIDTH), blk(GLA_VAL_WIDTH), blk(GLA_VAL_WIDTH),
                  blk(GLA_KEY_WIDTH), _const_spec((1, GLA_DV))],
        out_specs=blk(GLA_VAL_WIDTH),
        out_shape=jax.ShapeDtypeStruct((b, SEQ, GLA_VAL_WIDTH), BF16),
        scratch_shapes=[pltpu.VMEM((SEQ, GLA_VAL_WIDTH), F32), pltpu.VMEM((SEQ, GLA_KEY_WIDTH), BF16),
                        pltpu.VMEM((n_chunks, GLA_DV, GLA_KEY_WIDTH), F32),
                        pltpu.VMEM((n_chunks, 8, GLA_KEY_WIDTH), F32),
                        pltpu.VMEM((n_chunks, GLA_DV, GLA_KEY_WIDTH), BF16)],
        compiler_params=pltpu.CompilerParams(dimension_semantics=("arbitrary",), vmem_limit_bytes=VMEM_LIMIT),
        name="gla",
    )(qb, kb, vb, rb, la, ng)


def _mixx_body(x_ref, on_ref, og_ref, gm_ref, wn_ref, wg_ref, wo_ref, lx_ref, wq_ref, km_ref, vm_ref, wxo_ref,
               o_ref):
    gm = gm_ref[...].astype(F32)
    mix = (gm[:, :D_MODEL] * _dot(on_ref[...], wn_ref[...])
           + gm[:, D_MODEL:] * _dot(og_ref[...], wg_ref[...]))
    x1 = x_ref[...] + _dot(mix.astype(BF16), wo_ref[...])
    hx = _rms(x1, lx_ref[...]).astype(BF16)
    q = (_dot(hx, wq_ref[...]) * (XATTN_HEAD_DIM ** -0.5)).astype(BF16)
    outs = []
    for h in range(XATTN_HEADS):
        sl = slice(h * XATTN_HEAD_DIM, (h + 1) * XATTN_HEAD_DIM)
        s = _dot_nt(q[:, sl], km_ref[0, :, sl])
        e = jnp.exp(s - jnp.max(s, axis=-1, keepdims=True))
        l = jnp.sum(e, axis=-1, keepdims=True)
        outs.append((_dot(e.astype(BF16), vm_ref[0, :, sl]) / l).astype(BF16))
    o = jnp.concatenate(outs, axis=-1)
    o_ref[...] = x1 + _dot(o, wxo_ref[...])


def _mixx(x2, o_nsa, o_gla, gm, wn, wg, wo, lx, wq, km, vm, wxo):
    n = x2.shape[0]
    tiles_per_seq = SEQ // TM
    tile = lambda w: pl.BlockSpec((TM, w), lambda i: (i, 0))
    mblk = pl.BlockSpec((1, MEM_LEN, XATTN_WIDTH), lambda i: (i // tiles_per_seq, 0, 0))
    return pl.pallas_call(
        _mixx_body,
        grid=(n // TM,),
        in_specs=[tile(D_MODEL), tile(XATTN_WIDTH), tile(GLA_VAL_WIDTH), tile(2 * D_MODEL),
                  _const_spec((XATTN_WIDTH, D_MODEL)), _const_spec((GLA_VAL_WIDTH, D_MODEL)),
                  _const_spec((D_MODEL, D_MODEL)), _const_spec((1, D_MODEL)),
                  _const_spec((D_MODEL, XATTN_WIDTH)), mblk, mblk, _const_spec((XATTN_WIDTH, D_MODEL))],
        out_specs=tile(D_MODEL),
        out_shape=jax.ShapeDtypeStruct((n, D_MODEL), F32),
        compiler_params=pltpu.CompilerParams(dimension_semantics=("arbitrary",), vmem_limit_bytes=VMEM_LIMIT),
        name="mixx",
    )(x2, o_nsa, o_gla, gm, wn, wg, wo, lx, wq, km, vm, wxo)


def _ffn_body(x_ref, lf_ref, wu_ref, wgt_ref, cw_ref, cb_ref, wd_ref, lfin_ref, o_ref, ubuf_ref, carry_ref):
    @pl.when(pl.program_id(0) % (SEQ // TM) == 0)
    def _():
        carry_ref[...] = jnp.zeros_like(carry_ref)

    x = x_ref[...]
    hf = _rms(x, lf_ref[...]).astype(BF16)
    acc = jnp.zeros((TM, D_MODEL), F32)
    for c in range(FFN_DIM // FFN_CHUNK):
        cols = slice(c * FFN_CHUNK, (c + 1) * FFN_CHUNK)
        ubuf_ref[0:8, :] = carry_ref[:, cols]
        ubuf_ref[8:, :] = _dot(hf, wu_ref[:, cols])
        carry_ref[:, cols] = ubuf_ref[TM:TM + 8, :]
        u = (cw_ref[0:1, cols] * ubuf_ref[6:6 + TM, :] + cw_ref[1:2, cols] * ubuf_ref[7:7 + TM, :]
             + cw_ref[2:3, cols] * ubuf_ref[8:8 + TM, :] + cb_ref[:, cols])
        act = (jax.nn.gelu(u) * _dot(hf, wgt_ref[:, cols])).astype(BF16)
        acc = acc + _dot(act, wd_ref[cols, :])
    o_ref[...] = _rms(x + acc, lfin_ref[...])


def _ffn(x2, lf, wu, wgt, cw, cb, wd, lfin):
    n = x2.shape[0]
    tile = pl.BlockSpec((TM, D_MODEL), lambda i: (i, 0))
    return pl.pallas_call(
        _ffn_body,
        grid=(n // TM,),
        in_specs=[tile, _const_spec((1, D_MODEL)), _const_spec((D_MODEL, FFN_DIM)), _const_spec((D_MODEL, FFN_DIM)),
                  _const_spec((CONV_WIDTH, FFN_DIM)), _const_spec((1, FFN_DIM)), _const_spec((FFN_DIM, D_MODEL)),
                  _const_spec((1, D_MODEL))],
        out_specs=tile,
        out_shape=jax.ShapeDtypeStruct((n, D_MODEL), F32),
        scratch_shapes=[pltpu.VMEM((TM + 8, FFN_CHUNK), F32), pltpu.VMEM((8, FFN_DIM), F32)],
        compiler_params=pltpu.CompilerParams(dimension_semantics=("arbitrary",), vmem_limit_bytes=VMEM_LIMIT),
        name="ffn",
    )(x2, lf, wu, wgt, cw, cb, wd, lfin)


def _layout_inproj(w_in):
    o = 0
    seg = {}
    for name, wdt in (("q", 512), ("kc", 128), ("vc", 128), ("ks", 128), ("vs", 128), ("kw", 128), ("vw", 128),
                      ("gt", 24), ("qb", 256), ("kb", 256), ("vb", 512), ("rb", 512), ("al", 16), ("mg", 2048)):
        seg[name] = w_in[:, o:o + wdt]
        o += wdt
    q_scale = (HEAD_DIM ** -0.5) * LOG2E
    cols = [seg["q"] * q_scale, seg["ks"], seg["kw"], seg["vs"], seg["vw"], seg["kc"], seg["vc"]]
    cols += [jnp.pad(seg["gt"], ((0, 0), (0, LANE - 3 * NSA_HEADS)))]
    cols += [jnp.pad(seg["al"], ((0, 0), (0, LANE - GLA_RANK)))]
    cols += [seg["qb"] * (GLA_DK ** -0.5), seg["kb"], seg["vb"], seg["rb"]]
    cols += [seg["mg"]]
    return jnp.concatenate(cols, axis=1).astype(BF16)


def _position_constants():
    slopes = 2.0 ** (-np.arange(1, NSA_HEADS + 1, dtype=np.float64)) * LOG2E
    qconst = np.zeros((1, NSA_HEADS * LANE), np.float32)
    for hh in range(NSA_HEADS):
        qconst[0, hh * LANE + AUG_HI] = slopes[hh]
        qconst[0, hh * LANE + AUG_LO] = slopes[hh]
    t = np.arange(SEQ)
    kaug_s = np.zeros((SEQ, 2 * LANE), np.float32)
    kaug_w = np.zeros((SEQ, 2 * LANE), np.float32)
    for h in range(NSA_KV_HEADS):
        kaug_s[t, h * LANE + AUG_SEL + t // SLC_BLOCK] = 1.0
        for a in (kaug_s, kaug_w):
            a[:, h * LANE + AUG_HI] = (t // SLC_BLOCK) * SLC_BLOCK
            a[:, h * LANE + AUG_LO] = t % SLC_BLOCK
    slope_rows = np.zeros((NSA_KV_HEADS, 8, LANE), np.float32)
    rrep = np.zeros((NSA_KV_HEADS, LANE, 6 * LANE), np.float32)
    for h in range(NSA_KV_HEADS):
        for g in range(NSA_GROUP):
            slope_rows[h, g, :] = slopes[h * NSA_GROUP + g]
            for c in range(3):
                pp, half = g // 2, g % 2
                c0 = (c * 2 + pp) * LANE + half * HEAD_DIM
                rrep[h, (h * NSA_GROUP + g) * 3 + c, c0:c0 + HEAD_DIM] = 1.0
    ovl = np.zeros((N_SLC, N_CMP_PAD), np.float32)
    for n in range(N_CMP_PAD - 1):
        for tok in range(n * CMP_STRIDE, n * CMP_STRIDE + CMP_BLOCK):
            ovl[tok // SLC_BLOCK, n] += 1.0 / CMP_BLOCK
    return qconst, kaug_s, kaug_w, slope_rows, rrep, ovl


def _layout_compress(w1, w2, pos, dup):
    w1 = w1.reshape(2, CMP_STRIDE, 1, HEAD_DIM, CMP_HIDDEN)
    sel = jnp.eye(NSA_KV_HEADS, dtype=F32)[:, None, None, :, None, None]
    wh = (w1[None] * sel).reshape(NSA_KV_HEADS, 2, CMP_STRIDE * NSA_KV_HEADS * HEAD_DIM, CMP_HIDDEN)
    p = jnp.broadcast_to(pos.reshape(2, CMP_STRIDE, 1, HEAD_DIM), (2, CMP_STRIDE, NSA_KV_HEADS, HEAD_DIM))
    p = p.reshape(2, 1, CMP_STRIDE * NSA_KV_HEADS * HEAD_DIM)
    w2p = jnp.concatenate([w2, w2 if dup else jnp.zeros_like(w2)], axis=1)
    return wh[:, 0].astype(BF16), wh[:, 1].astype(BF16), w2p.astype(BF16), p[0], p[1]


def kernel(x, mem, ln_mix_g, w_in, nsa_gate_b, cmp_pos_k, cmp_w1_k, cmp_w2_k, cmp_pos_v, cmp_w1_v, cmp_w2_v,
           gla_w_alpha2, gla_b_alpha, gla_norm_g, w_branch_nsa, w_branch_gla, w_out, ln_x_g, ln_mem_g, w_xq,
           w_xkv, w_xo, ln_ffn_g, w_up, conv_w, conv_b, w_down, ln_final_g):
    b = x.shape[0]
    n = b * SEQ
    row = lambda v: v.reshape(1, -1).astype(F32)
    qconst, kaug_s, kaug_w, slope_rows, rrep, ovl = _position_constants()

    x2 = x.reshape(n, D_MODEL)
    gate_b = jnp.pad(nsa_gate_b[0], (0, LANE - 3 * NSA_HEADS)).reshape(1, LANE)
    wa2 = jnp.pad(gla_w_alpha2[0], ((0, LANE - GLA_RANK), (0, 0)))
    (q, ks, kw, vs, vw, kc, vc, gt, qb, kb, vb, rb, la, gm) = _inproj(
        x2, row(ln_mix_g[0]), _layout_inproj(w_in[0]), qconst, kaug_s, kaug_w, gate_b, wa2, row(gla_b_alpha[0]))

    s3 = lambda a: a.reshape(b, SEQ, a.shape[-1])
    chunked = lambda a: a.reshape(b, N_CMP_PAD, CMP_STRIDE * NSA_KV_HEADS * HEAD_DIM)
    wak, wbk, w2k, p1k, p2k = _layout_compress(cmp_w1_k[0], cmp_w2_k[0], cmp_pos_k[0], dup=False)
    wav, wbv, w2v, p1v, p2v = _layout_compress(cmp_w1_v[0], cmp_w2_v[0], cmp_pos_v[0], dup=True)
    o_nsa = _nsa(s3(q), s3(ks), s3(kw), s3(vs), s3(vw), chunked(kc), chunked(vc), s3(gt),
                 wak, wbk, w2k, wav, wbv, w2v, p1k, p2k, p1v, p2v,
                 jnp.asarray(rrep, BF16), jnp.asarray(slope_rows), jnp.asarray(ovl, BF16))

    o_gla = _gla(s3(qb), s3(kb), s3(vb), s3(rb), s3(la), row(gla_norm_g[0]))

    km, vm = _memkv(mem, row(ln_mem_g[0]), w_xkv[0].astype(BF16))
    x_mid = _mixx(x2, o_nsa.reshape(n, -1), o_gla.reshape(n, -1), gm,
                  w_branch_nsa[0].astype(BF16), w_branch_gla[0].astype(BF16), w_out[0].astype(BF16),
                  row(ln_x_g[0]), w_xq[0].astype(BF16), km, vm, w_xo[0].astype(BF16))

    out = _ffn(x_mid, row(ln_ffn_g[0]), w_up[0][:, :FFN_DIM].astype(BF16), w_up[0][:, FFN_DIM:].astype(BF16),
               conv_w[0], conv_b[0].reshape(1, FFN_DIM), w_down[0].astype(BF16), row(ln_final_g))
    return out.reshape(b, SEQ, D_MODEL)
```

```python
import math

import jax
import jax.numpy as jnp
import numpy as np
from jax import lax
from jax.experimental import pallas as pl
from jax.experimental.pallas import tpu as pltpu

F32 = jnp.float32
BF16 = jnp.bfloat16

D_MODEL = 1024
SEQ = 2048
MEM_LEN = 256
HEAD_DIM = 64
NSA_HEADS = 8
NSA_KV_HEADS = 2
NSA_GROUP = 4
CMP_BLOCK = 32
CMP_STRIDE = 16
CMP_HIDDEN = 128
N_CMP_PAD = SEQ // CMP_STRIDE
SLC_BLOCK = 64
N_SLC = SEQ // SLC_BLOCK
N_SELECT = 8
WINDOW = 256
GLA_HEADS = 4
GLA_DK = 64
GLA_DV = 128
GLA_KEY_WIDTH = 256
GLA_VAL_WIDTH = 512
GLA_RANK = 16
GLA_TAU = 16.0
GLA_CHUNK = 64
GLA_GROUP = 256
XATTN_HEADS = 4
XATTN_HEAD_DIM = 128
XATTN_WIDTH = 512
FFN_DIM = 2816
CONV_WIDTH = 3
RMS_EPS = 1e-6
NEG_INF = -1e30
LOG2E = math.log2(math.e)

LANE = 128
VMEM_LIMIT = 56 * 1024 * 1024

AUG_SEL = HEAD_DIM
AUG_HI = HEAD_DIM + N_SLC
AUG_LO = AUG_HI + 1

C_Q = 0
C_KS = C_Q + NSA_HEADS * HEAD_DIM
C_KW = C_KS + LANE
C_VS = C_KW + LANE
C_VW = C_VS + LANE
C_KC = C_VW + LANE
C_VC = C_KC + LANE
C_GT = C_VC + LANE
C_AL = C_GT + LANE
C_QB = C_AL + LANE
C_KB = C_QB + GLA_KEY_WIDTH
C_VB = C_KB + GLA_KEY_WIDTH
C_RB = C_VB + GLA_VAL_WIDTH
C_MG = C_RB + GLA_VAL_WIDTH
C_END = C_MG + 2 * D_MODEL

TM = 512
TM_WIDE = 1024
QUAD_TOKENS = 4 * SLC_BLOCK
QUAD_ROWS = NSA_GROUP * QUAD_TOKENS
WIN_PAD = WINDOW
WIN_KEYS = WIN_PAD + QUAD_TOKENS
KEY_CHUNK = 512
FFN_CHUNK = 1408


def _rms(x, g):
    return x * lax.rsqrt(jnp.mean(x * x, axis=-1, keepdims=True) + RMS_EPS) * g


def _dot(a, b, **kw):
    return jnp.dot(a, b, preferred_element_type=F32, **kw)


def _dot_nt(a, b):
    return lax.dot_general(a, b, (((1,), (1,)), ((), ())), preferred_element_type=F32)


def _dot_tn(a, b):
    return lax.dot_general(a, b, (((0,), (0,)), ((), ())), preferred_element_type=F32)


def _split_bf16(x):
    hi = x.astype(BF16)
    return hi, (x - hi.astype(F32)).astype(BF16)


def _const_spec(shape):
    return pl.BlockSpec(shape, lambda *_: (0,) * len(shape), pipeline_mode=pl.Buffered(1))


def _inproj_body(x_ref, g_ref, w_ref, qc_ref, kas_ref, kaw_ref, gb_ref, wa2_ref, ba_ref,
                 q_o, ks_o, kw_o, vs_o, vw_o, kc_o, vc_o, gt_o, qb_o, kb_o, vb_o, rb_o, la_o, gm_o, chunk_ref):
    h = _rms(x_ref[...], g_ref[...]).astype(BF16)

    def proj(lo, hi):
        return _dot(h, w_ref[:, lo:hi])

    low = lax.broadcasted_iota(jnp.int32, (TM_WIDE, LANE), 1) < HEAD_DIM

    def head_slots(x, fill):
        swapped = pltpu.roll(x, HEAD_DIM, axis=1)
        return [jnp.where(low, x, fill(0, x, swapped)), jnp.where(low, swapped, fill(1, swapped, x))]

    def group(wide, c0, base):
        return wide[:, c0 - base:c0 - base + LANE]

    pq = proj(C_Q, C_KS)
    for pr in range(NSA_HEADS // 2):
        slots = head_slots(group(pq, C_Q + pr * LANE, C_Q),
                           lambda i, mine, other: qc_ref[:, (2 * pr + i) * LANE:(2 * pr + i + 1) * LANE])
        for i, sl in enumerate(slots):
            q_o[:, (2 * pr + i) * LANE:(2 * pr + i + 1) * LANE] = sl.astype(BF16)
    pkv = proj(C_KS, C_KC)
    for c0, aug_ref, out in ((C_KS, kas_ref, ks_o), (C_KW, kaw_ref, kw_o)):
        slots = head_slots(group(pkv, c0, C_KS), lambda i, mine, other: aug_ref[:, i * LANE:(i + 1) * LANE])
        for i, sl in enumerate(slots):
            out[:, i * LANE:(i + 1) * LANE] = sl.astype(BF16)
    for c0, out in ((C_VS, vs_o), (C_VW, vw_o)):
        for i, sl in enumerate(head_slots(group(pkv, c0, C_KS), lambda i, mine, other: other)):
            out[:, i * LANE:(i + 1) * LANE] = sl.astype(BF16)
    pmisc = proj(C_KC, C_QB)
    for c0, out in ((C_KC, kc_o), (C_VC, vc_o)):
        chunk_ref[...] = group(pmisc, c0, C_KC)
        for tk in range(CMP_STRIDE):
            out[:, tk * LANE:(tk + 1) * LANE] = chunk_ref[pl.ds(tk, TM_WIDE // CMP_STRIDE, stride=CMP_STRIDE), :].astype(BF16)
    gt_o[...] = jax.nn.sigmoid(group(pmisc, C_GT, C_KC) + gb_ref[...]).astype(BF16)
    qb_o[...] = proj(C_QB, C_KB).astype(BF16)
    kb_o[...] = proj(C_KB, C_VB).astype(BF16)
    vb_o[...] = proj(C_VB, C_RB).astype(BF16)
    rb_o[...] = proj(C_RB, C_MG).astype(BF16)
    z = _dot(group(pmisc, C_AL, C_KC), wa2_ref[...], precision=lax.Precision.HIGHEST) + ba_ref[...]
    log_sig = jnp.minimum(z, 0.0) - jnp.log1p(jnp.exp(-jnp.abs(z)))
    la_o[...] = log_sig * (1.0 / GLA_TAU)
    gm_o[...] = jax.nn.sigmoid(proj(C_MG, C_END)).astype(BF16)


def _inproj(x2, g, wcat, qconst, kaug_s, kaug_w, gate_b, wa2, ba):
    n = x2.shape[0]
    tiles_per_seq = SEQ // TM_WIDE
    outs = [(1, NSA_HEADS * LANE, BF16), (1, LANE * 2, BF16), (1, LANE * 2, BF16), (1, LANE * 2, BF16),
            (1, LANE * 2, BF16), (CMP_STRIDE, CMP_STRIDE * LANE, BF16), (CMP_STRIDE, CMP_STRIDE * LANE, BF16),
            (1, LANE, BF16), (1, GLA_KEY_WIDTH, BF16), (1, GLA_KEY_WIDTH, BF16),
            (1, GLA_VAL_WIDTH, BF16), (1, GLA_VAL_WIDTH, BF16), (1, GLA_KEY_WIDTH, F32), (1, 2 * D_MODEL, BF16)]
    tile = lambda w, per=1: pl.BlockSpec((TM_WIDE // per, w), lambda i: (i, 0))
    return pl.pallas_call(
        _inproj_body,
        grid=(n // TM_WIDE,),
        in_specs=[tile(D_MODEL), _const_spec((1, D_MODEL)), _const_spec((D_MODEL, C_END)),
                  _const_spec((1, NSA_HEADS * LANE)),
                  pl.BlockSpec((TM_WIDE, 2 * LANE), lambda i: (i % tiles_per_seq, 0)),
                  pl.BlockSpec((TM_WIDE, 2 * LANE), lambda i: (i % tiles_per_seq, 0)),
                  _const_spec((1, LANE)), _const_spec((LANE, GLA_KEY_WIDTH)), _const_spec((1, GLA_KEY_WIDTH))],
        out_specs=[tile(w, per) for per, w, _ in outs],
        out_shape=[jax.ShapeDtypeStruct((n // per, w), dt) for per, w, dt in outs],
        scratch_shapes=[pltpu.VMEM((TM_WIDE, LANE), F32)],
        compiler_params=pltpu.CompilerParams(dimension_semantics=("arbitrary",), vmem_limit_bytes=VMEM_LIMIT),
        name="inproj",
    )(x2, g, wcat, qconst, kaug_s, kaug_w, gate_b, wa2, ba)


def _memkv_body(m_ref, g_ref, w_ref, k_o, v_o):
    hm = _rms(m_ref[0], g_ref[...]).astype(BF16)
    kv = _dot(hm, w_ref[...])
    k_o[0] = kv[:, :XATTN_WIDTH].astype(BF16)
    v_o[0] = kv[:, XATTN_WIDTH:].astype(BF16)


def _memkv(mem, g, w):
    b = mem.shape[0]
    blk = pl.BlockSpec((1, MEM_LEN, XATTN_WIDTH), lambda i: (i, 0, 0))
    return pl.pallas_call(
        _memkv_body,
        grid=(b,),
        in_specs=[pl.BlockSpec((1, MEM_LEN, D_MODEL), lambda i: (i, 0, 0)), _const_spec((1, D_MODEL)),
                  _const_spec((D_MODEL, 2 * XATTN_WIDTH))],
        out_specs=[blk, blk],
        out_shape=[jax.ShapeDtypeStruct((b, MEM_LEN, XATTN_WIDTH), BF16)] * 2,
        compiler_params=pltpu.CompilerParams(dimension_semantics=("arbitrary",)),
        name="memkv",
    )(mem, g, w)


def _compress(c_ref, wa_ref, wb_ref, w2_ref, p1_ref, p2_ref):
    c = c_ref[0].astype(F32)
    a = _dot((c + p1_ref[...]).astype(BF16), wa_ref[0])
    b = _dot((c + p2_ref[...]).astype(BF16), wb_ref[0])
    hid = a + pltpu.roll(b, N_CMP_PAD - 1, axis=0)
    return _dot(jax.nn.gelu(hid).astype(BF16), w2_ref[...]).astype(BF16)


def _nsa_constants(slope_ref, biasc_ref, wbias_ref, dbias_ref):
    t_col = lax.broadcasted_iota(jnp.int32, (SEQ, LANE), 0)
    n_row = lax.broadcasted_iota(jnp.int32, (SEQ, LANE), 1)
    dist_c = (t_col - (n_row * CMP_STRIDE + (CMP_BLOCK - 1))).astype(F32)
    for g in range(NSA_GROUP):
        biasc_ref[g] = jnp.where(dist_c >= 0.0, -slope_ref[0, g:g + 1, :] * dist_c, NEG_INF)
    col = lax.broadcasted_iota(jnp.int32, (QUAD_ROWS, WIN_KEYS), 1)
    row = lax.broadcasted_iota(jnp.int32, (QUAD_ROWS, WIN_KEYS), 0)
    dist_w = ((row >> 8) << 6) + (row & (SLC_BLOCK - 1)) + WIN_PAD - col
    wbias_ref[...] = jnp.where((dist_w >= 0) & (dist_w < WINDOW), 0.0, NEG_INF)
    col = lax.broadcasted_iota(jnp.int32, (QUAD_ROWS, QUAD_TOKENS), 1)
    row = lax.broadcasted_iota(jnp.int32, (QUAD_ROWS, QUAD_TOKENS), 0)
    blk_r, blk_c = row >> 8, col >> 6
    causal = (col & (SLC_BLOCK - 1)) <= (row & (SLC_BLOCK - 1))
    dbias_ref[...] = jnp.where((blk_c < blk_r) | ((blk_c == blk_r) & causal), 0.0, NEG_INF)


def _nsa_body(q_ref, ks_ref, kw_ref, vs_ref, vw_ref, kc_ref, vc_ref, gt_ref,
              wak_ref, wbk_ref, w2k_ref, wav_ref, wbv_ref, w2v_ref, p1k_ref, p2k_ref, p1v_ref, p2v_ref,
              rrep_ref, slope_ref, ovl_ref, o_ref,
              ocmp_ref, qs_ref, kwp_ref, vwp_ref, biasc_ref, wbias_ref, dbias_ref, s_ref, e_ref, sw_ref, ew_ref):
    @pl.when(pl.program_id(1) == 0)
    def _():
        _nsa_constants(slope_ref, biasc_ref, wbias_ref, dbias_ref)

    kcmp = _compress(kc_ref, wak_ref, wbk_ref, w2k_ref, p1k_ref, p2k_ref)
    vcmp = _compress(vc_ref, wav_ref, wbv_ref, w2v_ref, p1v_ref, p2v_ref)

    low_half = lax.broadcasted_iota(jnp.int32, (SEQ, LANE), 1) < HEAD_DIM
    has_key = lax.broadcasted_iota(jnp.int32, (SEQ, 1), 0) >= CMP_BLOCK - 1
    imp = jnp.zeros((SEQ, LANE), F32)
    o_even = None
    for g in range(NSA_GROUP):
        s = _dot_nt(q_ref[0, :, g * LANE:(g + 1) * LANE], kcmp) + biasc_ref[g]
        e = jnp.exp2(s - jnp.max(s, axis=-1, keepdims=True))
        l = jnp.sum(e, axis=-1, keepdims=True)
        p = e * jnp.where(has_key, 1.0 / l, 0.0)
        imp = imp + p
        oc = _dot(p.astype(BF16), vcmp)
        if g % 2 == 0:
            o_even = oc
        else:
            ocmp_ref[g // 2] = jnp.where(low_half, o_even, oc)

    hi, lo = _split_bf16(imp.T)
    ovl = ovl_ref[...]
    imp_b = _dot(ovl, hi) + _dot(ovl, lo)
    j = lax.broadcasted_iota(jnp.int32, (N_SLC, SEQ), 0)
    cur = lax.broadcasted_iota(jnp.int32, (N_SLC, SEQ), 1) >> 6
    forced = (j == 0) | (j == cur) | (j == cur - 1)
    val = jnp.where(forced, float(NSA_GROUP + 1), jnp.where(j > cur, -1.0, imp_b))
    j8 = lax.broadcasted_iota(jnp.int32, (8, SEQ), 0)
    vals = [val[8 * r:8 * r + 8, :] for r in range(N_SLC // 8)]
    ranks = [jnp.zeros((8, SEQ), F32) for _ in vals]
    for jp in range(N_SLC):
        vj = val[jp:jp + 1, :]
        for r, vr in enumerate(vals):
            if 8 * r + 7 <= jp:
                beats = vj > vr
            elif 8 * r > jp:
                beats = vj >= vr
            else:
                beats = (vj > vr) | ((vj == vr) & (j8 > jp - 8 * r))
            ranks[r] = ranks[r] + jnp.where(beats, 1.0, 0.0)
    rank = jnp.concatenate(ranks, axis=0)
    sel_bias = jnp.where((rank < float(N_SELECT)) & (j <= cur), 0.0, NEG_INF)
    aug_t = jnp.concatenate([jnp.zeros((AUG_SEL, SEQ), F32), sel_bias,
                             jnp.zeros((LANE - AUG_SEL - N_SLC, SEQ), F32)], axis=0)
    aug = aug_t.T.astype(BF16)
    for g in range(NSA_GROUP):
        qs_ref[g] = q_ref[0, :, g * LANE:(g + 1) * LANE] + aug

    pad_lane = lax.broadcasted_iota(jnp.int32, (WIN_PAD, LANE), 1)
    kwp_ref[0:WIN_PAD, :] = jnp.where(pad_lane == AUG_HI, NEG_INF, 0.0).astype(BF16)
    vwp_ref[0:WIN_PAD, :] = jnp.zeros((WIN_PAD, LANE), BF16)
    kwp_ref[WIN_PAD:, :] = kw_ref[0]
    vwp_ref[WIN_PAD:, :] = vw_ref[0]

    lane64 = lax.broadcasted_iota(jnp.int32, (SLC_BLOCK, LANE), 1) < HEAD_DIM

    def softmax_pv(s_buf, e_buf, rows, nkeys, v):
        m = jnp.max(s_buf[rows, 0:nkeys], axis=-1, keepdims=True)
        e = jnp.exp2(s_buf[rows, 0:nkeys] - m)
        e_buf[rows, 0:nkeys] = e.astype(BF16)
        return _dot(e_buf[rows, 0:nkeys], v) * (1.0 / jnp.sum(e, axis=-1, keepdims=True))

    def quad(iq):
        r0 = iq * QUAD_TOKENS
        nkeys = r0 + QUAD_TOKENS
        qs = jnp.concatenate([qs_ref[g, r0 + bq * SLC_BLOCK:r0 + (bq + 1) * SLC_BLOCK, :]
                              for bq in range(QUAD_TOKENS // SLC_BLOCK) for g in range(NSA_GROUP)], axis=0)
        halves = [slice(hf * QUAD_ROWS // 2, (hf + 1) * QUAD_ROWS // 2) for hf in range(2)]
        for rows in halves:
            for c0 in range(0, r0, KEY_CHUNK):
                c1 = min(c0 + KEY_CHUNK, r0)
                s_ref[rows, c0:c1] = _dot_nt(qs[rows], ks_ref[0, c0:c1, :])
            s_ref[rows, r0:nkeys] = _dot_nt(qs[rows], ks_ref[0, r0:nkeys, :]) + dbias_ref[rows, :]
        sw_ref[...] = _dot_nt(qs, kwp_ref[r0:r0 + WIN_KEYS, :]) + wbias_ref[...]
        o_s = jnp.concatenate([softmax_pv(s_ref, e_ref, rows, nkeys, vs_ref[0, 0:nkeys, :]) for rows in halves],
                              axis=0)
        o_w = softmax_pv(sw_ref, ew_ref, slice(None), WIN_KEYS, vwp_ref[r0:r0 + WIN_KEYS, :])
        gt = _dot(gt_ref[0, r0:r0 + QUAD_TOKENS, :], rrep_ref[0])
        for bq in range(QUAD_TOKENS // SLC_BLOCK):
            tok = slice(r0 + bq * SLC_BLOCK, r0 + (bq + 1) * SLC_BLOCK)
            gtb = gt[bq * SLC_BLOCK:(bq + 1) * SLC_BLOCK]
            for pp in range(2):
                a0 = (bq * NSA_GROUP + 2 * pp) * SLC_BLOCK
                a1, a2 = a0 + SLC_BLOCK, a0 + 2 * SLC_BLOCK
                osl = jnp.where(lane64, o_s[a0:a1], o_s[a1:a2])
                owi = jnp.where(lane64, o_w[a0:a1], o_w[a1:a2])
                out = (gtb[:, pp * LANE:(pp + 1) * LANE] * ocmp_ref[pp, tok, :]
                       + gtb[:, (2 + pp) * LANE:(3 + pp) * LANE] * osl
                       + gtb[:, (4 + pp) * LANE:(5 + pp) * LANE] * owi)
                o_ref[0, tok, pp * LANE:(pp + 1) * LANE] = out.astype(BF16)

    one = jnp.minimum(pl.program_id(1) + 1, 1)
    for iq in range(SEQ // QUAD_TOKENS):
        def body(_, carry, iq=iq):
            quad(iq)
            return carry

        lax.fori_loop(0, one, body, 0)


def _nsa(q, ks, kw, vs, vw, kc2, vc2, gt, wak, wbk, w2k, wav, wbv, w2v, p1k, p2k, p1v, p2v, rrep, slopes, ovl):
    b = q.shape[0]
    hs = lambda w: pl.BlockSpec((1, SEQ, w), lambda h, i: (i, 0, h))
    perh = lambda s: pl.BlockSpec((1,) + s, lambda h, i: (h,) + (0,) * len(s))
    cblk = pl.BlockSpec((1, N_CMP_PAD, CMP_STRIDE * 2 * HEAD_DIM), lambda h, i: (i, 0, 0))
    w1 = (CMP_STRIDE * 2 * HEAD_DIM, CMP_HIDDEN)
    return pl.pallas_call(
        _nsa_body,
        grid=(NSA_KV_HEADS, b),
        in_specs=[hs(NSA_GROUP * LANE), hs(LANE), hs(LANE), hs(LANE), hs(LANE), cblk, cblk,
                  pl.BlockSpec((1, SEQ, LANE), lambda h, i: (i, 0, 0)),
                  perh(w1), perh(w1), _const_spec((CMP_HIDDEN, LANE)),
                  perh(w1), perh(w1), _const_spec((CMP_HIDDEN, LANE)),
                  _const_spec((1, w1[0])), _const_spec((1, w1[0])), _const_spec((1, w1[0])), _const_spec((1, w1[0])),
                  perh((LANE, 6 * LANE)), perh((8, LANE)), _const_spec((N_SLC, LANE))],
        out_specs=pl.BlockSpec((1, SEQ, 2 * LANE), lambda h, i: (i, 0, h)),
        out_shape=jax.ShapeDtypeStruct((b, SEQ, NSA_HEADS * HEAD_DIM), BF16),
        scratch_shapes=[pltpu.VMEM((2, SEQ, LANE), F32), pltpu.VMEM((NSA_GROUP, SEQ, LANE), BF16),
                        pltpu.VMEM((SEQ + WIN_PAD, LANE), BF16), pltpu.VMEM((SEQ + WIN_PAD, LANE), BF16),
                        pltpu.VMEM((NSA_GROUP, SEQ, LANE), F32), pltpu.VMEM((QUAD_ROWS, WIN_KEYS), F32),
                        pltpu.VMEM((QUAD_ROWS, QUAD_TOKENS), F32),
                        pltpu.VMEM((QUAD_ROWS, SEQ), F32), pltpu.VMEM((QUAD_ROWS, SEQ), BF16),
                        pltpu.VMEM((QUAD_ROWS, WIN_KEYS), F32), pltpu.VMEM((QUAD_ROWS, WIN_KEYS), BF16)],
        compiler_params=pltpu.CompilerParams(dimension_semantics=("arbitrary", "arbitrary"),
                                             vmem_limit_bytes=VMEM_LIMIT),
        name="nsa",
    )(q, ks, kw, vs, vw, kc2, vc2, gt, wak, wbk, w2k, wav, wbv, w2v, p1k, p2k, p1v, p2v, rrep, slopes, ovl)


def _gla_body(q_ref, k_ref, v_ref, r_ref, la_ref, ng_ref, o_ref, oi_ref, qd_ref, inc_ref, dec_ref, st_ref):
    n_chunks = SEQ // GLA_CHUNK
    per_group = GLA_GROUP // GLA_CHUNK
    ri = lax.broadcasted_iota(jnp.int32, (GLA_GROUP, GLA_GROUP), 0)
    ci = lax.broadcasted_iota(jnp.int32, (GLA_GROUP, GLA_GROUP), 1)
    same_chunk = (ri >> 6) == (ci >> 6)
    tri = same_chunk & (ci <= ri)
    t_mat = jnp.where(tri, 1.0, 0.0).astype(BF16)
    o_mat = jnp.where(same_chunk, 1.0, 0.0).astype(BF16)
    head_g = lax.broadcasted_iota(jnp.int32, (GLA_GROUP, GLA_KEY_WIDTH), 1) >> 6
    head_c = lax.broadcasted_iota(jnp.int32, (GLA_CHUNK, GLA_KEY_WIDTH), 1) >> 6
    head_s = lax.broadcasted_iota(jnp.int32, (GLA_DV, GLA_KEY_WIDTH), 1) >> 6

    def group(gi, carry):
        r0 = pl.multiple_of(gi * GLA_GROUP, GLA_GROUP)
        rows = pl.ds(r0, GLA_GROUP)
        hi, lo = _split_bf16(la_ref[0, rows, :])
        b = _dot(t_mat, hi) + _dot(t_mat, lo)
        b_last = _dot(o_mat, hi) + _dot(o_mat, lo)
        k = k_ref[0, rows, :].astype(F32)
        q_d = q_ref[0, rows, :].astype(F32) * jnp.exp(b)
        k_d = (k * jnp.exp(-b)).astype(BF16)
        k_s = (k * jnp.exp(b_last - b)).astype(BF16)
        qd_ref[rows, :] = q_d.astype(BF16)
        v = v_ref[0, rows, :]
        for h in range(GLA_HEADS):
            qm = jnp.where(head_g == h, q_d, 0.0).astype(BF16)
            att = jnp.where(tri, _dot_nt(qm, k_d), 0.0)
            oi_ref[rows, h * GLA_DV:(h + 1) * GLA_DV] = _dot(att.astype(BF16), v[:, h * GLA_DV:(h + 1) * GLA_DV])
        decay = jnp.exp(b_last)
        for c in range(per_group):
            cr = slice(c * GLA_CHUNK, (c + 1) * GLA_CHUNK)
            inc_full = _dot_tn(v[cr, :], k_s[cr, :])
            inc = jnp.zeros((GLA_DV, GLA_KEY_WIDTH), F32)
            for h in range(GLA_HEADS):
                inc = inc + jnp.where(head_s == h, inc_full[h * GLA_DV:(h + 1) * GLA_DV, :], 0.0)
            inc_ref[gi * per_group + c] = inc
            dec_ref[gi * per_group + c] = decay[c * GLA_CHUNK:c * GLA_CHUNK + 8, :]
        return carry

    lax.fori_loop(0, SEQ // GLA_GROUP, group, 0, unroll=4)

    state = jnp.zeros((GLA_DV, GLA_KEY_WIDTH), F32)
    for n in range(n_chunks):
        st_ref[n] = state.astype(BF16)
        state = state * dec_ref[n, 0:1, :] + inc_ref[n]

    def emit(gi, carry):
        r0 = pl.multiple_of(gi * GLA_GROUP, GLA_GROUP)
        for c in range(per_group):
            rows = pl.ds(r0 + c * GLA_CHUNK, GLA_CHUNK)
            q_d = qd_ref[rows, :]
            st = st_ref[gi * per_group + c]
            for h in range(GLA_HEADS):
                cols = slice(h * GLA_DV, (h + 1) * GLA_DV)
                qm = jnp.where(head_c == h, q_d, jnp.zeros_like(q_d))
                o = oi_ref[rows, cols] + _dot_nt(qm, st)
                o = o * lax.rsqrt(jnp.mean(o * o, axis=-1, keepdims=True) + RMS_EPS) * ng_ref[...]
                r = r_ref[0, rows, cols].astype(F32)
                o_ref[0, rows, cols] = (o * (r * jax.nn.sigmoid(r))).astype(BF16)
        return carry

    lax.fori_loop(0, SEQ // GLA_GROUP, emit, 0, unroll=4)


def _gla(qb, kb, vb, rb, la, ng):
    b = qb.shape[0]
    n_chunks = SEQ // GLA_CHUNK
    blk = lambda w: pl.BlockSpec((1, SEQ, w), lambda i: (i, 0, 0))
    return pl.pallas_call(
        _gla_body,
        grid=(b,),
        in_specs=[blk(GLA_KEY_WIDTH), blk(GLA_KEY_WIDTH), blk(GLA_VAL_WIDTH), blk(GLA_VAL_WIDTH),
                  blk(GLA_KEY_WIDTH), _const_spec((1, GLA_DV))],
        out_specs=blk(GLA_VAL_WIDTH),
        out_shape=jax.ShapeDtypeStruct((b, SEQ, GLA_VAL_WIDTH), BF16),
        scratch_shapes=[pltpu.VMEM((SEQ, GLA_VAL_WIDTH), F32), pltpu.VMEM((SEQ, GLA_KEY_WIDTH), BF16),
                        pltpu.VMEM((n_chunks, GLA_DV, GLA_KEY_WIDTH), F32),
                        pltpu.VMEM((n_chunks, 8, GLA_KEY_WIDTH), F32),
                        pltpu.VMEM((n_chunks, GLA_DV, GLA_KEY_WIDTH), BF16)],
        compiler_params=pltpu.CompilerParams(dimension_semantics=("arbitrary",), vmem_limit_bytes=VMEM_LIMIT),
        name="gla",
    )(qb, kb, vb, rb, la, ng)


def _mixx_body(x_ref, on_ref, og_ref, gm_ref, wn_ref, wg_ref, wo_ref, lx_ref, wq_ref, km_ref, vm_ref, wxo_ref,
               o_ref):
    gm = gm_ref[...].astype(F32)
    mix = (gm[:, :D_MODEL] * _dot(on_ref[...], wn_ref[...])
           + gm[:, D_MODEL:] * _dot(og_ref[...], wg_ref[...]))
    x1 = x_ref[...] + _dot(mix.astype(BF16), wo_ref[...])
    hx = _rms(x1, lx_ref[...]).astype(BF16)
    q = (_dot(hx, wq_ref[...]) * (XATTN_HEAD_DIM ** -0.5)).astype(BF16)
    outs = []
    for h in range(XATTN_HEADS):
        sl = slice(h * XATTN_HEAD_DIM, (h + 1) * XATTN_HEAD_DIM)
        s = _dot_nt(q[:, sl], km_ref[0, :, sl])
        e = jnp.exp(s - jnp.max(s, axis=-1, keepdims=True))
        l = jnp.sum(e, axis=-1, keepdims=True)
        outs.append((_dot(e.astype(BF16), vm_ref[0, :, sl]) / l).astype(BF16))
    o = jnp.concatenate(outs, axis=-1)
    o_ref[...] = x1 + _dot(o, wxo_ref[...])


def _mixx(x2, o_nsa, o_gla, gm, wn, wg, wo, lx, wq, km, vm, wxo):
    n = x2.shape[0]
    tiles_per_seq = SEQ // TM_WIDE
    tile = lambda w: pl.BlockSpec((TM_WIDE, w), lambda i: (i, 0))
    mblk = pl.BlockSpec((1, MEM_LEN, XATTN_WIDTH), lambda i: (i // tiles_per_seq, 0, 0))
    return pl.pallas_call(
        _mixx_body,
        grid=(n // TM_WIDE,),
        in_specs=[tile(D_MODEL), tile(XATTN_WIDTH), tile(GLA_VAL_WIDTH), tile(2 * D_MODEL),
                  _const_spec((XATTN_WIDTH, D_MODEL)), _const_spec((GLA_VAL_WIDTH, D_MODEL)),
                  _const_spec((D_MODEL, D_MODEL)), _const_spec((1, D_MODEL)),
                  _const_spec((D_MODEL, XATTN_WIDTH)), mblk, mblk, _const_spec((XATTN_WIDTH, D_MODEL))],
        out_specs=tile(D_MODEL),
        out_shape=jax.ShapeDtypeStruct((n, D_MODEL), F32),
        compiler_params=pltpu.CompilerParams(dimension_semantics=("arbitrary",), vmem_limit_bytes=VMEM_LIMIT),
        name="mixx",
    )(x2, o_nsa, o_gla, gm, wn, wg, wo, lx, wq, km, vm, wxo)


def _ffn_body(x_ref, lf_ref, wu_ref, wgt_ref, cw_ref, cb_ref, wd_ref, lfin_ref, o_ref, ubuf_ref, carry_ref):
    @pl.when(pl.program_id(0) % (SEQ // TM) == 0)
    def _():
        carry_ref[...] = jnp.zeros_like(carry_ref)

    x = x_ref[...]
    hf = _rms(x, lf_ref[...]).astype(BF16)
    acc = jnp.zeros((TM, D_MODEL), F32)
    for c in range(FFN_DIM // FFN_CHUNK):
        cols = slice(c * FFN_CHUNK, (c + 1) * FFN_CHUNK)
        ubuf_ref[0:8, :] = carry_ref[:, cols]
        ubuf_ref[8:, :] = _dot(hf, wu_ref[:, cols])
        carry_ref[:, cols] = ubuf_ref[TM:TM + 8, :]
        u = (cw_ref[0:1, cols] * ubuf_ref[6:6 + TM, :] + cw_ref[1:2, cols] * ubuf_ref[7:7 + TM, :]
             + cw_ref[2:3, cols] * ubuf_ref[8:8 + TM, :] + cb_ref[:, cols])
        act = (jax.nn.gelu(u) * _dot(hf, wgt_ref[:, cols])).astype(BF16)
        acc = acc + _dot(act, wd_ref[cols, :])
    o_ref[...] = _rms(x + acc, lfin_ref[...])


def _ffn(x2, lf, wu, wgt, cw, cb, wd, lfin):
    n = x2.shape[0]
    tile = pl.BlockSpec((TM, D_MODEL), lambda i: (i, 0))
    return pl.pallas_call(
        _ffn_body,
        grid=(n // TM,),
        in_specs=[tile, _const_spec((1, D_MODEL)), _const_spec((D_MODEL, FFN_DIM)), _const_spec((D_MODEL, FFN_DIM)),
                  _const_spec((CONV_WIDTH, FFN_DIM)), _const_spec((1, FFN_DIM)), _const_spec((FFN_DIM, D_MODEL)),
                  _const_spec((1, D_MODEL))],
        out_specs=tile,
        out_shape=jax.ShapeDtypeStruct((n, D_MODEL), F32),
        scratch_shapes=[pltpu.VMEM((TM + 8, FFN_CHUNK), F32), pltpu.VMEM((8, FFN_DIM), F32)],
        compiler_params=pltpu.CompilerParams(dimension_semantics=("arbitrary",), vmem_limit_bytes=VMEM_LIMIT),
        name="ffn",
    )(x2, lf, wu, wgt, cw, cb, wd, lfin)


def _layout_inproj(w_in):
    o = 0
    seg = {}
    for name, wdt in (("q", 512), ("kc", 128), ("vc", 128), ("ks", 128), ("vs", 128), ("kw", 128), ("vw", 128),
                      ("gt", 24), ("qb", 256), ("kb", 256), ("vb", 512), ("rb", 512), ("al", 16), ("mg", 2048)):
        seg[name] = w_in[:, o:o + wdt]
        o += wdt
    q_scale = (HEAD_DIM ** -0.5) * LOG2E
    cols = [seg["q"] * q_scale, seg["ks"], seg["kw"], seg["vs"], seg["vw"], seg["kc"], seg["vc"]]
    cols += [jnp.pad(seg["gt"], ((0, 0), (0, LANE - 3 * NSA_HEADS)))]
    cols += [jnp.pad(seg["al"], ((0, 0), (0, LANE - GLA_RANK)))]
    cols += [seg["qb"] * (GLA_DK ** -0.5), seg["kb"], seg["vb"], seg["rb"]]
    cols += [seg["mg"]]
    return jnp.concatenate(cols, axis=1).astype(BF16)


def _position_constants():
    slopes = 2.0 ** (-np.arange(1, NSA_HEADS + 1, dtype=np.float64)) * LOG2E
    qconst = np.zeros((1, NSA_HEADS * LANE), np.float32)
    for hh in range(NSA_HEADS):
        qconst[0, hh * LANE + AUG_HI] = slopes[hh]
        qconst[0, hh * LANE + AUG_LO] = slopes[hh]
    t = np.arange(SEQ)
    kaug_s = np.zeros((SEQ, 2 * LANE), np.float32)
    kaug_w = np.zeros((SEQ, 2 * LANE), np.float32)
    for h in range(NSA_KV_HEADS):
        kaug_s[t, h * LANE + AUG_SEL + t // SLC_BLOCK] = 1.0
        for a in (kaug_s, kaug_w):
            a[:, h * LANE + AUG_HI] = (t // SLC_BLOCK) * SLC_BLOCK
            a[:, h * LANE + AUG_LO] = t % SLC_BLOCK
    slope_rows = np.zeros((NSA_KV_HEADS, 8, LANE), np.float32)
    rrep = np.zeros((NSA_KV_HEADS, LANE, 6 * LANE), np.float32)
    for h in range(NSA_KV_HEADS):
        for g in range(NSA_GROUP):
            slope_rows[h, g, :] = slopes[h * NSA_GROUP + g]
            for c in range(3):
                pp, half = g // 2, g % 2
                c0 = (c * 2 + pp) * LANE + half * HEAD_DIM
                rrep[h, (h * NSA_GROUP + g) * 3 + c, c0:c0 + HEAD_DIM] = 1.0
    ovl = np.zeros((N_SLC, N_CMP_PAD), np.float32)
    for n in range(N_CMP_PAD - 1):
        for tok in range(n * CMP_STRIDE, n * CMP_STRIDE + CMP_BLOCK):
            ovl[tok // SLC_BLOCK, n] += 1.0 / CMP_BLOCK
    return qconst, kaug_s, kaug_w, slope_rows, rrep, ovl


def _layout_compress(w1, w2, pos, dup):
    w1 = w1.reshape(2, CMP_STRIDE, 1, HEAD_DIM, CMP_HIDDEN)
    sel = jnp.eye(NSA_KV_HEADS, dtype=F32)[:, None, None, :, None, None]
    wh = (w1[None] * sel).reshape(NSA_KV_HEADS, 2, CMP_STRIDE * NSA_KV_HEADS * HEAD_DIM, CMP_HIDDEN)
    p = jnp.broadcast_to(pos.reshape(2, CMP_STRIDE, 1, HEAD_DIM), (2, CMP_STRIDE, NSA_KV_HEADS, HEAD_DIM))
    p = p.reshape(2, 1, CMP_STRIDE * NSA_KV_HEADS * HEAD_DIM)
    w2p = jnp.concatenate([w2, w2 if dup else jnp.zeros_like(w2)], axis=1)
    return wh[:, 0].astype(BF16), wh[:, 1].astype(BF16), w2p.astype(BF16), p[0], p[1]


def kernel(x, mem, ln_mix_g, w_in, nsa_gate_b, cmp_pos_k, cmp_w1_k, cmp_w2_k, cmp_pos_v, cmp_w1_v, cmp_w2_v,
           gla_w_alpha2, gla_b_alpha, gla_norm_g, w_branch_nsa, w_branch_gla, w_out, ln_x_g, ln_mem_g, w_xq,
           w_xkv, w_xo, ln_ffn_g, w_up, conv_w, conv_b, w_down, ln_final_g):
    b = x.shape[0]
    n = b * SEQ
    row = lambda v: v.reshape(1, -1).astype(F32)
    qconst, kaug_s, kaug_w, slope_rows, rrep, ovl = _position_constants()

    x2 = x.reshape(n, D_MODEL)
    gate_b = jnp.pad(nsa_gate_b[0], (0, LANE - 3 * NSA_HEADS)).reshape(1, LANE)
    wa2 = jnp.pad(gla_w_alpha2[0], ((0, LANE - GLA_RANK), (0, 0)))
    (q, ks, kw, vs, vw, kc, vc, gt, qb, kb, vb, rb, la, gm) = _inproj(
        x2, row(ln_mix_g[0]), _layout_inproj(w_in[0]), qconst, kaug_s, kaug_w, gate_b, wa2, row(gla_b_alpha[0]))

    s3 = lambda a: a.reshape(b, SEQ, a.shape[-1])
    chunked = lambda a: a.reshape(b, N_CMP_PAD, a.shape[-1])
    wak, wbk, w2k, p1k, p2k = _layout_compress(cmp_w1_k[0], cmp_w2_k[0], cmp_pos_k[0], dup=False)
    wav, wbv, w2v, p1v, p2v = _layout_compress(cmp_w1_v[0], cmp_w2_v[0], cmp_pos_v[0], dup=True)
    o_nsa = _nsa(s3(q), s3(ks), s3(kw), s3(vs), s3(vw), chunked(kc), chunked(vc), s3(gt),
                 wak, wbk, w2k, wav, wbv, w2v, p1k, p2k, p1v, p2v,
                 jnp.asarray(rrep, BF16), jnp.asarray(slope_rows), jnp.asarray(ovl, BF16))

    o_gla = _gla(s3(qb), s3(kb), s3(vb), s3(rb), s3(la), row(gla_norm_g[0]))

    km, vm = _memkv(mem, row(ln_mem_g[0]), w_xkv[0].astype(BF16))
    x_mid = _mixx(x2, o_nsa.reshape(n, -1), o_gla.reshape(n, -1), gm,
                  w_branch_nsa[0].astype(BF16), w_branch_gla[0].astype(BF16), w_out[0].astype(BF16),
                  row(ln_x_g[0]), w_xq[0].astype(BF16), km, vm, w_xo[0].astype(BF16))

    out = _ffn(x_mid, row(ln_ffn_g[0]), w_up[0][:, :FFN_DIM].astype(BF16), w_up[0][:, FFN_DIM:].astype(BF16),
               conv_w[0], conv_b[0].reshape(1, FFN_DIM), w_down[0].astype(BF16), row(ln_final_g))
    return out.reshape(b, SEQ, D_MODEL)
```

```python
import math

import jax
import jax.numpy as jnp
import numpy as np
from jax import lax
from jax.experimental import pallas as pl
from jax.experimental.pallas import tpu as pltpu

F32 = jnp.float32
BF16 = jnp.bfloat16

D_MODEL = 1024
SEQ = 2048
MEM_LEN = 256
HEAD_DIM = 64
NSA_HEADS = 8
NSA_KV_HEADS = 2
NSA_GROUP = 4
CMP_BLOCK = 32
CMP_STRIDE = 16
CMP_HIDDEN = 128
N_CMP_PAD = SEQ // CMP_STRIDE
SLC_BLOCK = 64
N_SLC = SEQ // SLC_BLOCK
N_SELECT = 8
WINDOW = 256
GLA_HEADS = 4
GLA_DK = 64
GLA_DV = 128
GLA_KEY_WIDTH = 256
GLA_VAL_WIDTH = 512
GLA_RANK = 16
GLA_TAU = 16.0
GLA_CHUNK = 64
GLA_GROUP = 256
XATTN_HEADS = 4
XATTN_HEAD_DIM = 128
XATTN_WIDTH = 512
FFN_DIM = 2816
CONV_WIDTH = 3
RMS_EPS = 1e-6
NEG_INF = -1e30
LOG2E = math.log2(math.e)

LANE = 128
VMEM_LIMIT = 56 * 1024 * 1024

AUG_SEL = HEAD_DIM
AUG_HI = HEAD_DIM + N_SLC
AUG_LO = AUG_HI + 1

C_Q = 0
C_KS = C_Q + NSA_HEADS * HEAD_DIM
C_KW = C_KS + LANE
C_VS = C_KW + LANE
C_VW = C_VS + LANE
C_KC = C_VW + LANE
C_VC = C_KC + LANE
C_GT = C_VC + LANE
C_AL = C_GT + LANE
C_QB = C_AL + LANE
C_KB = C_QB + GLA_KEY_WIDTH
C_VB = C_KB + GLA_KEY_WIDTH
C_RB = C_VB + GLA_VAL_WIDTH
C_MG = C_RB + GLA_VAL_WIDTH
C_END = C_MG + 2 * D_MODEL

TM = 512
TM_WIDE = 1024
QUAD_TOKENS = 4 * SLC_BLOCK
QUAD_ROWS = NSA_GROUP * QUAD_TOKENS
WIN_PAD = WINDOW
WIN_KEYS = WIN_PAD + QUAD_TOKENS
KEY_CHUNK = 512
MXU_DIM = 256
FFN_CHUNK = 6 * MXU_DIM


def _rms(x, g):
    return x * lax.rsqrt(jnp.mean(x * x, axis=-1, keepdims=True) + RMS_EPS) * g


def _dot(a, b, **kw):
    return jnp.dot(a, b, preferred_element_type=F32, **kw)


def _dot_nt(a, b):
    return lax.dot_general(a, b, (((1,), (1,)), ((), ())), preferred_element_type=F32)


def _dot_tn(a, b):
    return lax.dot_general(a, b, (((0,), (0,)), ((), ())), preferred_element_type=F32)


def _split_bf16(x):
    hi = x.astype(BF16)
    return hi, (x - hi.astype(F32)).astype(BF16)


def _const_spec(shape):
    return pl.BlockSpec(shape, lambda *_: (0,) * len(shape), pipeline_mode=pl.Buffered(1))


def _inproj_body(x_ref, g_ref, w_ref, qc_ref, kas_ref, kaw_ref, gb_ref, wa2_ref, ba_ref,
                 q_o, ks_o, kw_o, vs_o, vw_o, kc_o, vc_o, gt_o, qb_o, kb_o, vb_o, rb_o, la_o, gm_o, chunk_ref):
    h = _rms(x_ref[...], g_ref[...]).astype(BF16)

    def proj(lo, hi):
        return _dot(h, w_ref[:, lo:hi])

    low = lax.broadcasted_iota(jnp.int32, (TM_WIDE, LANE), 1) < HEAD_DIM

    def head_slots(x, fill):
        swapped = pltpu.roll(x, HEAD_DIM, axis=1)
        return [jnp.where(low, x, fill(0, x, swapped)), jnp.where(low, swapped, fill(1, swapped, x))]

    def group(wide, c0, base):
        return wide[:, c0 - base:c0 - base + LANE]

    pq = proj(C_Q, C_KS)
    for pr in range(NSA_HEADS // 2):
        slots = head_slots(group(pq, C_Q + pr * LANE, C_Q),
                           lambda i, mine, other: qc_ref[:, (2 * pr + i) * LANE:(2 * pr + i + 1) * LANE])
        for i, sl in enumerate(slots):
            q_o[:, (2 * pr + i) * LANE:(2 * pr + i + 1) * LANE] = sl.astype(BF16)
    pkv = proj(C_KS, C_KC)
    for c0, aug_ref, out in ((C_KS, kas_ref, ks_o), (C_KW, kaw_ref, kw_o)):
        slots = head_slots(group(pkv, c0, C_KS), lambda i, mine, other: aug_ref[:, i * LANE:(i + 1) * LANE])
        for i, sl in enumerate(slots):
            out[:, i * LANE:(i + 1) * LANE] = sl.astype(BF16)
    for c0, out in ((C_VS, vs_o), (C_VW, vw_o)):
        for i, sl in enumerate(head_slots(group(pkv, c0, C_KS), lambda i, mine, other: other)):
            out[:, i * LANE:(i + 1) * LANE] = sl.astype(BF16)
    pmisc = proj(C_KC, C_QB)
    for c0, out in ((C_KC, kc_o), (C_VC, vc_o)):
        chunk_ref[...] = group(pmisc, c0, C_KC)
        for tk in range(CMP_STRIDE):
            out[:, tk * LANE:(tk + 1) * LANE] = chunk_ref[pl.ds(tk, TM_WIDE // CMP_STRIDE, stride=CMP_STRIDE), :].astype(BF16)
    gt_o[...] = jax.nn.sigmoid(group(pmisc, C_GT, C_KC) + gb_ref[...]).astype(BF16)
    qb_o[...] = proj(C_QB, C_KB).astype(BF16)
    kb_o[...] = proj(C_KB, C_VB).astype(BF16)
    vb_o[...] = proj(C_VB, C_RB).astype(BF16)
    rb_o[...] = proj(C_RB, C_MG).astype(BF16)
    z = _dot(group(pmisc, C_AL, C_KC).astype(BF16), wa2_ref[...]) + ba_ref[...]
    log_sig = jnp.minimum(z, 0.0) - jnp.log1p(jnp.exp(-jnp.abs(z)))
    la_o[...] = log_sig * (1.0 / GLA_TAU)
    gm_o[...] = jax.nn.sigmoid(proj(C_MG, C_END)).astype(BF16)


def _inproj(x2, g, wcat, qconst, kaug_s, kaug_w, gate_b, wa2, ba):
    n = x2.shape[0]
    tiles_per_seq = SEQ // TM_WIDE
    outs = [(1, NSA_HEADS * LANE, BF16), (1, LANE * 2, BF16), (1, LANE * 2, BF16), (1, LANE * 2, BF16),
            (1, LANE * 2, BF16), (CMP_STRIDE, CMP_STRIDE * LANE, BF16), (CMP_STRIDE, CMP_STRIDE * LANE, BF16),
            (1, LANE, BF16), (1, GLA_KEY_WIDTH, BF16), (1, GLA_KEY_WIDTH, BF16),
            (1, GLA_VAL_WIDTH, BF16), (1, GLA_VAL_WIDTH, BF16), (1, GLA_KEY_WIDTH, F32), (1, 2 * D_MODEL, BF16)]
    tile = lambda w, per=1: pl.BlockSpec((TM_WIDE // per, w), lambda i: (i, 0))
    return pl.pallas_call(
        _inproj_body,
        grid=(n // TM_WIDE,),
        in_specs=[tile(D_MODEL), _const_spec((1, D_MODEL)), _const_spec((D_MODEL, C_END)),
                  _const_spec((1, NSA_HEADS * LANE)),
                  pl.BlockSpec((TM_WIDE, 2 * LANE), lambda i: (i % tiles_per_seq, 0)),
                  pl.BlockSpec((TM_WIDE, 2 * LANE), lambda i: (i % tiles_per_seq, 0)),
                  _const_spec((1, LANE)), _const_spec((LANE, GLA_KEY_WIDTH)), _const_spec((1, GLA_KEY_WIDTH))],
        out_specs=[tile(w, per) for per, w, _ in outs],
        out_shape=[jax.ShapeDtypeStruct((n // per, w), dt) for per, w, dt in outs],
        scratch_shapes=[pltpu.VMEM((TM_WIDE, LANE), F32)],
        compiler_params=pltpu.CompilerParams(dimension_semantics=("arbitrary",), vmem_limit_bytes=VMEM_LIMIT),
        name="inproj",
    )(x2, g, wcat, qconst, kaug_s, kaug_w, gate_b, wa2, ba)


def _memkv_body(m_ref, g_ref, w_ref, k_o, v_o):
    hm = _rms(m_ref[0], g_ref[...]).astype(BF16)
    kv = _dot(hm, w_ref[...])
    k_o[0] = kv[:, :XATTN_WIDTH].astype(BF16)
    v_o[0] = kv[:, XATTN_WIDTH:].astype(BF16)


def _memkv(mem, g, w):
    b = mem.shape[0]
    blk = pl.BlockSpec((1, MEM_LEN, XATTN_WIDTH), lambda i: (i, 0, 0))
    return pl.pallas_call(
        _memkv_body,
        grid=(b,),
        in_specs=[pl.BlockSpec((1, MEM_LEN, D_MODEL), lambda i: (i, 0, 0)), _const_spec((1, D_MODEL)),
                  _const_spec((D_MODEL, 2 * XATTN_WIDTH))],
        out_specs=[blk, blk],
        out_shape=[jax.ShapeDtypeStruct((b, MEM_LEN, XATTN_WIDTH), BF16)] * 2,
        compiler_params=pltpu.CompilerParams(dimension_semantics=("arbitrary",)),
        name="memkv",
    )(mem, g, w)


def _compress(c_ref, wa_ref, wb_ref, w2_ref, p1_ref, p2_ref):
    c = c_ref[0].astype(F32)
    a = _dot((c + p1_ref[...]).astype(BF16), wa_ref[0])
    b = _dot((c + p2_ref[...]).astype(BF16), wb_ref[0])
    hid = a + pltpu.roll(b, N_CMP_PAD - 1, axis=0)
    return _dot(jax.nn.gelu(hid).astype(BF16), w2_ref[...]).astype(BF16)


def _nsa_constants(slope_ref, biasc_ref, wbias_ref, dbias_ref):
    t_col = lax.broadcasted_iota(jnp.int32, (SEQ, LANE), 0)
    n_row = lax.broadcasted_iota(jnp.int32, (SEQ, LANE), 1)
    dist_c = (t_col - (n_row * CMP_STRIDE + (CMP_BLOCK - 1))).astype(F32)
    for g in range(NSA_GROUP):
        biasc_ref[g] = jnp.where(dist_c >= 0.0, -slope_ref[0, g:g + 1, :] * dist_c, NEG_INF)
    col = lax.broadcasted_iota(jnp.int32, (QUAD_ROWS, WIN_KEYS), 1)
    row = lax.broadcasted_iota(jnp.int32, (QUAD_ROWS, WIN_KEYS), 0)
    dist_w = ((row >> 8) << 6) + (row & (SLC_BLOCK - 1)) + WIN_PAD - col
    wbias_ref[...] = jnp.where((dist_w >= 0) & (dist_w < WINDOW), 0.0, NEG_INF)
    col = lax.broadcasted_iota(jnp.int32, (QUAD_ROWS, QUAD_TOKENS), 1)
    row = lax.broadcasted_iota(jnp.int32, (QUAD_ROWS, QUAD_TOKENS), 0)
    blk_r, blk_c = row >> 8, col >> 6
    causal = (col & (SLC_BLOCK - 1)) <= (row & (SLC_BLOCK - 1))
    dbias_ref[...] = jnp.where((blk_c < blk_r) | ((blk_c == blk_r) & causal), 0.0, NEG_INF)


def _nsa_body(q_ref, ks_ref, kw_ref, vs_ref, vw_ref, kc_ref, vc_ref, gt_ref,
              wak_ref, wbk_ref, w2k_ref, wav_ref, wbv_ref, w2v_ref, p1k_ref, p2k_ref, p1v_ref, p2v_ref,
              rrep_ref, slope_ref, ovl_ref, o_ref,
              ocmp_ref, qs_ref, kwp_ref, vwp_ref, biasc_ref, wbias_ref, dbias_ref, s_ref, e_ref, sw_ref, ew_ref):
    @pl.when(pl.program_id(1) == 0)
    def _():
        _nsa_constants(slope_ref, biasc_ref, wbias_ref, dbias_ref)

    kcmp = _compress(kc_ref, wak_ref, wbk_ref, w2k_ref, p1k_ref, p2k_ref)
    vcmp = _compress(vc_ref, wav_ref, wbv_ref, w2v_ref, p1v_ref, p2v_ref)

    low_half = lax.broadcasted_iota(jnp.int32, (SEQ, LANE), 1) < HEAD_DIM
    has_key = lax.broadcasted_iota(jnp.int32, (SEQ, 1), 0) >= CMP_BLOCK - 1
    imp = jnp.zeros((SEQ, LANE), F32)
    o_even = None
    for g in range(NSA_GROUP):
        s = _dot_nt(q_ref[0, :, g * LANE:(g + 1) * LANE], kcmp) + biasc_ref[g]
        e = jnp.exp2(s - jnp.max(s, axis=-1, keepdims=True))
        l = jnp.sum(e, axis=-1, keepdims=True)
        p = e * jnp.where(has_key, 1.0 / l, 0.0)
        imp = imp + p
        oc = _dot(p.astype(BF16), vcmp)
        if g % 2 == 0:
            o_even = oc
        else:
            ocmp_ref[g // 2] = jnp.where(low_half, o_even, oc)

    hi, lo = _split_bf16(imp.T)
    ovl = ovl_ref[...]
    imp_b = _dot(ovl, hi) + _dot(ovl, lo)
    j = lax.broadcasted_iota(jnp.int32, (N_SLC, SEQ), 0)
    cur = lax.broadcasted_iota(jnp.int32, (N_SLC, SEQ), 1) >> 6
    forced = (j == 0) | (j == cur) | (j == cur - 1)
    val = jnp.where(forced, float(NSA_GROUP + 1), jnp.where(j > cur, -1.0, imp_b))
    j8 = lax.broadcasted_iota(jnp.int32, (8, SEQ), 0)
    vals = [val[8 * r:8 * r + 8, :] for r in range(N_SLC // 8)]
    ranks = [jnp.zeros((8, SEQ), F32) for _ in vals]
    for jp in range(N_SLC):
        vj = val[jp:jp + 1, :]
        for r, vr in enumerate(vals):
            if 8 * r + 7 <= jp:
                beats = vj > vr
            elif 8 * r > jp:
                beats = vj >= vr
            else:
                beats = (vj > vr) | ((vj == vr) & (j8 > jp - 8 * r))
            ranks[r] = ranks[r] + jnp.where(beats, 1.0, 0.0)
    rank = jnp.concatenate(ranks, axis=0)
    sel_bias = jnp.where((rank < float(N_SELECT)) & (j <= cur), 0.0, NEG_INF)
    aug_t = jnp.concatenate([jnp.zeros((AUG_SEL, SEQ), F32), sel_bias,
                             jnp.zeros((LANE - AUG_SEL - N_SLC, SEQ), F32)], axis=0)
    aug = aug_t.T.astype(BF16)
    for g in range(NSA_GROUP):
        qs_ref[g] = q_ref[0, :, g * LANE:(g + 1) * LANE] + aug

    pad_lane = lax.broadcasted_iota(jnp.int32, (WIN_PAD, LANE), 1)
    kwp_ref[0:WIN_PAD, :] = jnp.where(pad_lane == AUG_HI, NEG_INF, 0.0).astype(BF16)
    vwp_ref[0:WIN_PAD, :] = jnp.zeros((WIN_PAD, LANE), BF16)
    kwp_ref[WIN_PAD:, :] = kw_ref[0]
    vwp_ref[WIN_PAD:, :] = vw_ref[0]

    lane64 = lax.broadcasted_iota(jnp.int32, (SLC_BLOCK, LANE), 1) < HEAD_DIM

    def softmax_pv(s_buf, e_buf, rows, nkeys, v):
        m = jnp.max(s_buf[rows, 0:nkeys], axis=-1, keepdims=True)
        e = jnp.exp2(s_buf[rows, 0:nkeys] - m)
        e_buf[rows, 0:nkeys] = e.astype(BF16)
        return _dot(e_buf[rows, 0:nkeys], v) * (1.0 / jnp.sum(e, axis=-1, keepdims=True))

    def quad(iq):
        r0 = iq * QUAD_TOKENS
        nkeys = r0 + QUAD_TOKENS
        qs = jnp.concatenate([qs_ref[g, r0 + bq * SLC_BLOCK:r0 + (bq + 1) * SLC_BLOCK, :]
                              for bq in range(QUAD_TOKENS // SLC_BLOCK) for g in range(NSA_GROUP)], axis=0)
        halves = [slice(hf * QUAD_ROWS // 2, (hf + 1) * QUAD_ROWS // 2) for hf in range(2)]
        for rows in halves:
            for c0 in range(0, r0, KEY_CHUNK):
                c1 = min(c0 + KEY_CHUNK, r0)
                s_ref[rows, c0:c1] = _dot_nt(qs[rows], ks_ref[0, c0:c1, :])
            s_ref[rows, r0:nkeys] = _dot_nt(qs[rows], ks_ref[0, r0:nkeys, :]) + dbias_ref[rows, :]
        sw_ref[...] = _dot_nt(qs, kwp_ref[r0:r0 + WIN_KEYS, :]) + wbias_ref[...]
        o_s = jnp.concatenate([softmax_pv(s_ref, e_ref, rows, nkeys, vs_ref[0, 0:nkeys, :]) for rows in halves],
                              axis=0)
        o_w = softmax_pv(sw_ref, ew_ref, slice(None), WIN_KEYS, vwp_ref[r0:r0 + WIN_KEYS, :])
        gt = _dot(gt_ref[0, r0:r0 + QUAD_TOKENS, :], rrep_ref[0])
        for bq in range(QUAD_TOKENS // SLC_BLOCK):
            tok = slice(r0 + bq * SLC_BLOCK, r0 + (bq + 1) * SLC_BLOCK)
            gtb = gt[bq * SLC_BLOCK:(bq + 1) * SLC_BLOCK]
            for pp in range(2):
                a0 = (bq * NSA_GROUP + 2 * pp) * SLC_BLOCK
                a1, a2 = a0 + SLC_BLOCK, a0 + 2 * SLC_BLOCK
                osl = jnp.where(lane64, o_s[a0:a1], o_s[a1:a2])
                owi = jnp.where(lane64, o_w[a0:a1], o_w[a1:a2])
                out = (gtb[:, pp * LANE:(pp + 1) * LANE] * ocmp_ref[pp, tok, :]
                       + gtb[:, (2 + pp) * LANE:(3 + pp) * LANE] * osl
                       + gtb[:, (4 + pp) * LANE:(5 + pp) * LANE] * owi)
                o_ref[0, tok, pp * LANE:(pp + 1) * LANE] = out.astype(BF16)

    one = jnp.minimum(pl.program_id(1) + 1, 1)
    for iq in range(SEQ // QUAD_TOKENS):
        def body(_, carry, iq=iq):
            quad(iq)
            return carry

        lax.fori_loop(0, one, body, 0)


def _nsa(q, ks, kw, vs, vw, kc2, vc2, gt, wak, wbk, w2k, wav, wbv, w2v, p1k, p2k, p1v, p2v, rrep, slopes, ovl):
    b = q.shape[0]
    hs = lambda w: pl.BlockSpec((1, SEQ, w), lambda h, i: (i, 0, h))
    perh = lambda s: pl.BlockSpec((1,) + s, lambda h, i: (h,) + (0,) * len(s))
    cblk = pl.BlockSpec((1, N_CMP_PAD, CMP_STRIDE * 2 * HEAD_DIM), lambda h, i: (i, 0, 0))
    w1 = (CMP_STRIDE * 2 * HEAD_DIM, CMP_HIDDEN)
    return pl.pallas_call(
        _nsa_body,
        grid=(NSA_KV_HEADS, b),
        in_specs=[hs(NSA_GROUP * LANE), hs(LANE), hs(LANE), hs(LANE), hs(LANE), cblk, cblk,
                  pl.BlockSpec((1, SEQ, LANE), lambda h, i: (i, 0, 0)),
                  perh(w1), perh(w1), _const_spec((CMP_HIDDEN, LANE)),
                  perh(w1), perh(w1), _const_spec((CMP_HIDDEN, LANE)),
                  _const_spec((1, w1[0])), _const_spec((1, w1[0])), _const_spec((1, w1[0])), _const_spec((1, w1[0])),
                  perh((LANE, 6 * LANE)), perh((8, LANE)), _const_spec((N_SLC, LANE))],
        out_specs=pl.BlockSpec((1, SEQ, 2 * LANE), lambda h, i: (i, 0, h)),
        out_shape=jax.ShapeDtypeStruct((b, SEQ, NSA_HEADS * HEAD_DIM), BF16),
        scratch_shapes=[pltpu.VMEM((2, SEQ, LANE), F32), pltpu.VMEM((NSA_GROUP, SEQ, LANE), BF16),
                        pltpu.VMEM((SEQ + WIN_PAD, LANE), BF16), pltpu.VMEM((SEQ + WIN_PAD, LANE), BF16),
                        pltpu.VMEM((NSA_GROUP, SEQ, LANE), F32), pltpu.VMEM((QUAD_ROWS, WIN_KEYS), F32),
                        pltpu.VMEM((QUAD_ROWS, QUAD_TOKENS), F32),
                        pltpu.VMEM((QUAD_ROWS, SEQ), F32), pltpu.VMEM((QUAD_ROWS, SEQ), BF16),
                        pltpu.VMEM((QUAD_ROWS, WIN_KEYS), F32), pltpu.VMEM((QUAD_ROWS, WIN_KEYS), BF16)],
        compiler_params=pltpu.CompilerParams(dimension_semantics=("arbitrary", "arbitrary"),
                                             vmem_limit_bytes=VMEM_LIMIT),
        name="nsa",
    )(q, ks, kw, vs, vw, kc2, vc2, gt, wak, wbk, w2k, wav, wbv, w2v, p1k, p2k, p1v, p2v, rrep, slopes, ovl)


def _gla_body(q_ref, k_ref, v_ref, r_ref, la_ref, ng_ref, o_ref, oi_ref, qd_ref, inc_ref, dec_ref, st_ref):
    n_chunks = SEQ // GLA_CHUNK
    per_group = GLA_GROUP // GLA_CHUNK
    ri = lax.broadcasted_iota(jnp.int32, (GLA_GROUP, GLA_GROUP), 0)
    ci = lax.broadcasted_iota(jnp.int32, (GLA_GROUP, GLA_GROUP), 1)
    same_chunk = (ri >> 6) == (ci >> 6)
    tri = same_chunk & (ci <= ri)
    t_mat = jnp.where(tri, 1.0, 0.0).astype(BF16)
    o_mat = jnp.where(same_chunk, 1.0, 0.0).astype(BF16)
    head_g = lax.broadcasted_iota(jnp.int32, (GLA_GROUP, GLA_KEY_WIDTH), 1) >> 6
    head_c = lax.broadcasted_iota(jnp.int32, (GLA_CHUNK, GLA_KEY_WIDTH), 1) >> 6
    head_s = lax.broadcasted_iota(jnp.int32, (GLA_DV, GLA_KEY_WIDTH), 1) >> 6

    def group(gi, carry):
        r0 = pl.multiple_of(gi * GLA_GROUP, GLA_GROUP)
        rows = pl.ds(r0, GLA_GROUP)
        hi, lo = _split_bf16(la_ref[0, rows, :])
        b = _dot(t_mat, hi) + _dot(t_mat, lo)
        b_last = _dot(o_mat, hi) + _dot(o_mat, lo)
        k = k_ref[0, rows, :].astype(F32)
        q_d = q_ref[0, rows, :].astype(F32) * jnp.exp(b)
        k_d = (k * jnp.exp(-b)).astype(BF16)
        k_s = (k * jnp.exp(b_last - b)).astype(BF16)
        qd_ref[rows, :] = q_d.astype(BF16)
        v = v_ref[0, rows, :]
        for h in range(GLA_HEADS):
            qm = jnp.where(head_g == h, q_d, 0.0).astype(BF16)
            att = jnp.where(tri, _dot_nt(qm, k_d), 0.0)
            oi_ref[rows, h * GLA_DV:(h + 1) * GLA_DV] = _dot(att.astype(BF16), v[:, h * GLA_DV:(h + 1) * GLA_DV])
        decay = jnp.exp(b_last)
        for c in range(per_group):
            cr = slice(c * GLA_CHUNK, (c + 1) * GLA_CHUNK)
            inc_full = _dot_tn(v[cr, :], k_s[cr, :])
            inc = jnp.zeros((GLA_DV, GLA_KEY_WIDTH), F32)
            for h in range(GLA_HEADS):
                inc = inc + jnp.where(head_s == h, inc_full[h * GLA_DV:(h + 1) * GLA_DV, :], 0.0)
            inc_ref[gi * per_group + c] = inc
            dec_ref[gi * per_group + c] = decay[c * GLA_CHUNK:c * GLA_CHUNK + 8, :]
        return carry

    lax.fori_loop(0, SEQ // GLA_GROUP, group, 0, unroll=8)

    state = jnp.zeros((GLA_DV, GLA_KEY_WIDTH), F32)
    for n in range(n_chunks):
        st_ref[n] = state.astype(BF16)
        state = state * dec_ref[n, 0:1, :] + inc_ref[n]

    def emit(gi, carry):
        r0 = pl.multiple_of(gi * GLA_GROUP, GLA_GROUP)
        for c in range(per_group):
            rows = pl.ds(r0 + c * GLA_CHUNK, GLA_CHUNK)
            q_d = qd_ref[rows, :]
            st = st_ref[gi * per_group + c]
            for h in range(GLA_HEADS):
                cols = slice(h * GLA_DV, (h + 1) * GLA_DV)
                qm = jnp.where(head_c == h, q_d, jnp.zeros_like(q_d))
                o = oi_ref[rows, cols] + _dot_nt(qm, st)
                o = o * lax.rsqrt(jnp.mean(o * o, axis=-1, keepdims=True) + RMS_EPS) * ng_ref[...]
                r = r_ref[0, rows, cols].astype(F32)
                o_ref[0, rows, cols] = (o * (r * jax.nn.sigmoid(r))).astype(BF16)
        return carry

    lax.fori_loop(0, SEQ // GLA_GROUP, emit, 0, unroll=8)


def _gla(qb, kb, vb, rb, la, ng):
    b = qb.shape[0]
    n_chunks = SEQ // GLA_CHUNK
    blk = lambda w: pl.BlockSpec((1, SEQ, w), lambda i: (i, 0, 0))
    return pl.pallas_call(
        _gla_body,
        grid=(b,),
        in_specs=[blk(GLA_KEY_WIDTH), blk(GLA_KEY_WIDTH), blk(GLA_VAL_WIDTH), blk(GLA_VAL_WIDTH),
                  blk(GLA_KEY_WIDTH), _const_spec((1, GLA_DV))],
        out_specs=blk(GLA_VAL_WIDTH),
        out_shape=jax.ShapeDtypeStruct((b, SEQ, GLA_VAL_WIDTH), BF16),
        scratch_shapes=[pltpu.VMEM((SEQ, GLA_VAL_WIDTH), F32), pltpu.VMEM((SEQ, GLA_KEY_WIDTH), BF16),
                        pltpu.VMEM((n_chunks, GLA_DV, GLA_KEY_WIDTH), F32),
                        pltpu.VMEM((n_chunks, 8, GLA_KEY_WIDTH), F32),
                        pltpu.VMEM((n_chunks, GLA_DV, GLA_KEY_WIDTH), BF16)],
        compiler_params=pltpu.CompilerParams(dimension_semantics=("arbitrary",), vmem_limit_bytes=VMEM_LIMIT),
        name="gla",
    )(qb, kb, vb, rb, la, ng)


def _mixx_body(x_ref, on_ref, og_ref, gm_ref, wn_ref, wg_ref, wo_ref, lx_ref, wq_ref, km_ref, vm_ref, wxo_ref,
               o_ref):
    gm = gm_ref[...].astype(F32)
    mix = (gm[:, :D_MODEL] * _dot(on_ref[...], wn_ref[...])
           + gm[:, D_MODEL:] * _dot(og_ref[...], wg_ref[...]))
    x1 = x_ref[...] + _dot(mix.astype(BF16), wo_ref[...])
    hx = _rms(x1, lx_ref[...]).astype(BF16)
    q = (_dot(hx, wq_ref[...]) * (XATTN_HEAD_DIM ** -0.5)).astype(BF16)
    outs = []
    for h in range(XATTN_HEADS):
        sl = slice(h * XATTN_HEAD_DIM, (h + 1) * XATTN_HEAD_DIM)
        s = _dot_nt(q[:, sl], km_ref[0, :, sl])
        e = jnp.exp(s - jnp.max(s, axis=-1, keepdims=True))
        l = jnp.sum(e, axis=-1, keepdims=True)
        outs.append((_dot(e.astype(BF16), vm_ref[0, :, sl]) / l).astype(BF16))
    o = jnp.concatenate(outs, axis=-1)
    o_ref[...] = x1 + _dot(o, wxo_ref[...])


def _mixx(x2, o_nsa, o_gla, gm, wn, wg, wo, lx, wq, km, vm, wxo):
    n = x2.shape[0]
    tiles_per_seq = SEQ // TM_WIDE
    tile = lambda w: pl.BlockSpec((TM_WIDE, w), lambda i: (i, 0))
    mblk = pl.BlockSpec((1, MEM_LEN, XATTN_WIDTH), lambda i: (i // tiles_per_seq, 0, 0))
    return pl.pallas_call(
        _mixx_body,
        grid=(n // TM_WIDE,),
        in_specs=[tile(D_MODEL), tile(XATTN_WIDTH), tile(GLA_VAL_WIDTH), tile(2 * D_MODEL),
                  _const_spec((XATTN_WIDTH, D_MODEL)), _const_spec((GLA_VAL_WIDTH, D_MODEL)),
                  _const_spec((D_MODEL, D_MODEL)), _const_spec((1, D_MODEL)),
                  _const_spec((D_MODEL, XATTN_WIDTH)), mblk, mblk, _const_spec((XATTN_WIDTH, D_MODEL))],
        out_specs=tile(D_MODEL),
        out_shape=jax.ShapeDtypeStruct((n, D_MODEL), F32),
        compiler_params=pltpu.CompilerParams(dimension_semantics=("arbitrary",), vmem_limit_bytes=VMEM_LIMIT),
        name="mixx",
    )(x2, o_nsa, o_gla, gm, wn, wg, wo, lx, wq, km, vm, wxo)


def _ffn_body(x_ref, lf_ref, wu_ref, wgt_ref, cw_ref, cb_ref, wd_ref, lfin_ref, o_ref, ubuf_ref, carry_ref):
    @pl.when(pl.program_id(0) % (SEQ // TM) == 0)
    def _():
        carry_ref[...] = jnp.zeros_like(carry_ref)

    x = x_ref[...]
    hf = _rms(x, lf_ref[...]).astype(BF16)
    acc = jnp.zeros((TM, D_MODEL), F32)
    for c0 in range(0, FFN_DIM, FFN_CHUNK):
        cols = slice(c0, min(c0 + FFN_CHUNK, FFN_DIM))
        w = cols.stop - cols.start
        ubuf_ref[0:8, 0:w] = carry_ref[:, cols]
        ubuf_ref[8:, 0:w] = _dot(hf, wu_ref[:, cols])
        carry_ref[:, cols] = ubuf_ref[TM:TM + 8, 0:w]
        u = (cw_ref[0:1, cols] * ubuf_ref[6:6 + TM, 0:w] + cw_ref[1:2, cols] * ubuf_ref[7:7 + TM, 0:w]
             + cw_ref[2:3, cols] * ubuf_ref[8:8 + TM, 0:w] + cb_ref[:, cols])
        act = (jax.nn.gelu(u) * _dot(hf, wgt_ref[:, cols])).astype(BF16)
        acc = acc + _dot(act, wd_ref[cols, :])
    o_ref[...] = _rms(x + acc, lfin_ref[...])


def _ffn(x2, lf, wu, wgt, cw, cb, wd, lfin):
    n = x2.shape[0]
    tile = pl.BlockSpec((TM, D_MODEL), lambda i: (i, 0))
    return pl.pallas_call(
        _ffn_body,
        grid=(n // TM,),
        in_specs=[tile, _const_spec((1, D_MODEL)), _const_spec((D_MODEL, FFN_DIM)), _const_spec((D_MODEL, FFN_DIM)),
                  _const_spec((CONV_WIDTH, FFN_DIM)), _const_spec((1, FFN_DIM)), _const_spec((FFN_DIM, D_MODEL)),
                  _const_spec((1, D_MODEL))],
        out_specs=tile,
        out_shape=jax.ShapeDtypeStruct((n, D_MODEL), F32),
        scratch_shapes=[pltpu.VMEM((TM + 8, FFN_CHUNK), F32), pltpu.VMEM((8, FFN_DIM), F32)],
        compiler_params=pltpu.CompilerParams(dimension_semantics=("arbitrary",), vmem_limit_bytes=VMEM_LIMIT),
        name="ffn",
    )(x2, lf, wu, wgt, cw, cb, wd, lfin)


def _layout_inproj(w_in):
    o = 0
    seg = {}
    for name, wdt in (("q", 512), ("kc", 128), ("vc", 128), ("ks", 128), ("vs", 128), ("kw", 128), ("vw", 128),
                      ("gt", 24), ("qb", 256), ("kb", 256), ("vb", 512), ("rb", 512), ("al", 16), ("mg", 2048)):
        seg[name] = w_in[:, o:o + wdt]
        o += wdt
    q_scale = (HEAD_DIM ** -0.5) * LOG2E
    cols = [seg["q"] * q_scale, seg["ks"], seg["kw"], seg["vs"], seg["vw"], seg["kc"], seg["vc"]]
    cols += [jnp.pad(seg["gt"], ((0, 0), (0, LANE - 3 * NSA_HEADS)))]
    cols += [jnp.pad(seg["al"], ((0, 0), (0, LANE - GLA_RANK)))]
    cols += [seg["qb"] * (GLA_DK ** -0.5), seg["kb"], seg["vb"], seg["rb"]]
    cols += [seg["mg"]]
    return jnp.concatenate(cols, axis=1).astype(BF16)


def _position_constants():
    slopes = 2.0 ** (-np.arange(1, NSA_HEADS + 1, dtype=np.float64)) * LOG2E
    qconst = np.zeros((1, NSA_HEADS * LANE), np.float32)
    for hh in range(NSA_HEADS):
        qconst[0, hh * LANE + AUG_HI] = slopes[hh]
        qconst[0, hh * LANE + AUG_LO] = slopes[hh]
    t = np.arange(SEQ)
    kaug_s = np.zeros((SEQ, 2 * LANE), np.float32)
    kaug_w = np.zeros((SEQ, 2 * LANE), np.float32)
    for h in range(NSA_KV_HEADS):
        kaug_s[t, h * LANE + AUG_SEL + t // SLC_BLOCK] = 1.0
        for a in (kaug_s, kaug_w):
            a[:, h * LANE + AUG_HI] = (t // SLC_BLOCK) * SLC_BLOCK
            a[:, h * LANE + AUG_LO] = t % SLC_BLOCK
    slope_rows = np.zeros((NSA_KV_HEADS, 8, LANE), np.float32)
    rrep = np.zeros((NSA_KV_HEADS, LANE, 6 * LANE), np.float32)
    for h in range(NSA_KV_HEADS):
        for g in range(NSA_GROUP):
            slope_rows[h, g, :] = slopes[h * NSA_GROUP + g]
            for c in range(3):
                pp, half = g // 2, g % 2
                c0 = (c * 2 + pp) * LANE + half * HEAD_DIM
                rrep[h, (h * NSA_GROUP + g) * 3 + c, c0:c0 + HEAD_DIM] = 1.0
    ovl = np.zeros((N_SLC, N_CMP_PAD), np.float32)
    for n in range(N_CMP_PAD - 1):
        for tok in range(n * CMP_STRIDE, n * CMP_STRIDE + CMP_BLOCK):
            ovl[tok // SLC_BLOCK, n] += 1.0 / CMP_BLOCK
    return qconst, kaug_s, kaug_w, slope_rows, rrep, ovl


def _layout_compress(w1, w2, pos, dup):
    w1 = w1.reshape(2, CMP_STRIDE, 1, HEAD_DIM, CMP_HIDDEN)
    sel = jnp.eye(NSA_KV_HEADS, dtype=F32)[:, None, None, :, None, None]
    wh = (w1[None] * sel).reshape(NSA_KV_HEADS, 2, CMP_STRIDE * NSA_KV_HEADS * HEAD_DIM, CMP_HIDDEN)
    p = jnp.broadcast_to(pos.reshape(2, CMP_STRIDE, 1, HEAD_DIM), (2, CMP_STRIDE, NSA_KV_HEADS, HEAD_DIM))
    p = p.reshape(2, 1, CMP_STRIDE * NSA_KV_HEADS * HEAD_DIM)
    w2p = jnp.concatenate([w2, w2 if dup else jnp.zeros_like(w2)], axis=1)
    return wh[:, 0].astype(BF16), wh[:, 1].astype(BF16), w2p.astype(BF16), p[0], p[1]


def kernel(x, mem, ln_mix_g, w_in, nsa_gate_b, cmp_pos_k, cmp_w1_k, cmp_w2_k, cmp_pos_v, cmp_w1_v, cmp_w2_v,
           gla_w_alpha2, gla_b_alpha, gla_norm_g, w_branch_nsa, w_branch_gla, w_out, ln_x_g, ln_mem_g, w_xq,
           w_xkv, w_xo, ln_ffn_g, w_up, conv_w, conv_b, w_down, ln_final_g):
    b = x.shape[0]
    n = b * SEQ
    row = lambda v: v.reshape(1, -1).astype(F32)
    qconst, kaug_s, kaug_w, slope_rows, rrep, ovl = _position_constants()

    x2 = x.reshape(n, D_MODEL)
    gate_b = jnp.pad(nsa_gate_b[0], (0, LANE - 3 * NSA_HEADS)).reshape(1, LANE)
    wa2 = jnp.pad(gla_w_alpha2[0], ((0, LANE - GLA_RANK), (0, 0))).astype(BF16)
    (q, ks, kw, vs, vw, kc, vc, gt, qb, kb, vb, rb, la, gm) = _inproj(
        x2, row(ln_mix_g[0]), _layout_inproj(w_in[0]), qconst, kaug_s, kaug_w, gate_b, wa2, row(gla_b_alpha[0]))

    s3 = lambda a: a.reshape(b, SEQ, a.shape[-1])
    chunked = lambda a: a.reshape(b, N_CMP_PAD, a.shape[-1])
    wak, wbk, w2k, p1k, p2k = _layout_compress(cmp_w1_k[0], cmp_w2_k[0], cmp_pos_k[0], dup=False)
    wav, wbv, w2v, p1v, p2v = _layout_compress(cmp_w1_v[0], cmp_w2_v[0], cmp_pos_v[0], dup=True)
    o_nsa = _nsa(s3(q), s3(ks), s3(kw), s3(vs), s3(vw), chunked(kc), chunked(vc), s3(gt),
                 wak, wbk, w2k, wav, wbv, w2v, p1k, p2k, p1v, p2v,
                 jnp.asarray(rrep, BF16), jnp.asarray(slope_rows), jnp.asarray(ovl, BF16))

    o_gla = _gla(s3(qb), s3(kb), s3(vb), s3(rb), s3(la), row(gla_norm_g[0]))

    km, vm = _memkv(mem, row(ln_mem_g[0]), w_xkv[0].astype(BF16))
    x_mid = _mixx(x2, o_nsa.reshape(n, -1), o_gla.reshape(n, -1), gm,
                  w_branch_nsa[0].astype(BF16), w_branch_gla[0].astype(BF16), w_out[0].astype(BF16),
                  row(ln_x_g[0]), w_xq[0].astype(BF16), km, vm, w_xo[0].astype(BF16))

    out = _ffn(x_mid, row(ln_ffn_g[0]), w_up[0][:, :FFN_DIM].astype(BF16), w_up[0][:, FFN_DIM:].astype(BF16),
               conv_w[0], conv_b[0].reshape(1, FFN_DIM), w_down[0].astype(BF16), row(ln_final_g))
    return out.reshape(b, SEQ, D_MODEL)
```

```python
import math

import jax
import jax.numpy as jnp
import numpy as np
from jax import lax
from jax.experimental import pallas as pl
from jax.experimental.pallas import tpu as pltpu

F32 = jnp.float32
BF16 = jnp.bfloat16

D_MODEL = 1024
SEQ = 2048
MEM_LEN = 256
HEAD_DIM = 64
NSA_HEADS = 8
NSA_KV_HEADS = 2
NSA_GROUP = 4
CMP_BLOCK = 32
CMP_STRIDE = 16
CMP_HIDDEN = 128
N_CMP_PAD = SEQ // CMP_STRIDE
SLC_BLOCK = 64
N_SLC = SEQ // SLC_BLOCK
N_SELECT = 8
WINDOW = 256
GLA_HEADS = 4
GLA_DK = 64
GLA_DV = 128
GLA_KEY_WIDTH = 256
GLA_VAL_WIDTH = 512
GLA_RANK = 16
GLA_TAU = 16.0
GLA_CHUNK = 64
GLA_GROUP = 256
XATTN_HEADS = 4
XATTN_HEAD_DIM = 128
XATTN_WIDTH = 512
FFN_DIM = 2816
CONV_WIDTH = 3
RMS_EPS = 1e-6
NEG_INF = -1e30
LOG2E = math.log2(math.e)

LANE = 128
VMEM_LIMIT = 56 * 1024 * 1024

AUG_SEL = HEAD_DIM
AUG_HI = HEAD_DIM + N_SLC
AUG_LO = AUG_HI + 1

C_Q = 0
C_KS = C_Q + NSA_HEADS * HEAD_DIM
C_KW = C_KS + LANE
C_VS = C_KW + LANE
C_VW = C_VS + LANE
C_KC = C_VW + LANE
C_VC = C_KC + LANE
C_GT = C_VC + LANE
C_AL = C_GT + LANE
C_QB = C_AL + LANE
C_KB = C_QB + GLA_KEY_WIDTH
C_VB = C_KB + GLA_KEY_WIDTH
C_RB = C_VB + GLA_VAL_WIDTH
C_MG = C_RB + GLA_VAL_WIDTH
C_END = C_MG + 2 * D_MODEL

TM = 512
TM_WIDE = 1024
QUAD_TOKENS = 4 * SLC_BLOCK
QUAD_ROWS = NSA_GROUP * QUAD_TOKENS
WIN_PAD = WINDOW
WIN_KEYS = WIN_PAD + QUAD_TOKENS
KEY_CHUNK = 512
MXU_DIM = 256
FFN_CHUNK = 11 * MXU_DIM


def _rms(x, g):
    return x * lax.rsqrt(jnp.mean(x * x, axis=-1, keepdims=True) + RMS_EPS) * g


def _dot(a, b, **kw):
    return jnp.dot(a, b, preferred_element_type=F32, **kw)


def _dot_nt(a, b):
    return lax.dot_general(a, b, (((1,), (1,)), ((), ())), preferred_element_type=F32)


def _dot_tn(a, b):
    return lax.dot_general(a, b, (((0,), (0,)), ((), ())), preferred_element_type=F32)


def _split_bf16(x):
    hi = x.astype(BF16)
    return hi, (x - hi.astype(F32)).astype(BF16)


def _const_spec(shape):
    return pl.BlockSpec(shape, lambda *_: (0,) * len(shape), pipeline_mode=pl.Buffered(1))


def _inproj_body(x_ref, g_ref, w_ref, qc_ref, kas_ref, kaw_ref, gb_ref, wa2_ref, ba_ref,
                 q_o, ks_o, kw_o, vs_o, vw_o, kc_o, vc_o, gt_o, qb_o, kb_o, vb_o, rb_o, la_o, gm_o, chunk_ref):
    h = _rms(x_ref[...], g_ref[...]).astype(BF16)

    def proj(lo, hi):
        return _dot(h, w_ref[:, lo:hi])

    gm_o[...] = jax.nn.sigmoid(proj(C_MG, C_END)).astype(BF16)

    low = lax.broadcasted_iota(jnp.int32, (TM_WIDE, LANE), 1) < HEAD_DIM

    def head_slots(x, fill):
        swapped = pltpu.roll(x, HEAD_DIM, axis=1)
        return [jnp.where(low, x, fill(0, x, swapped)), jnp.where(low, swapped, fill(1, swapped, x))]

    def group(wide, c0, base):
        return wide[:, c0 - base:c0 - base + LANE]

    pq = proj(C_Q, C_KS)
    for pr in range(NSA_HEADS // 2):
        slots = head_slots(group(pq, C_Q + pr * LANE, C_Q),
                           lambda i, mine, other: qc_ref[:, (2 * pr + i) * LANE:(2 * pr + i + 1) * LANE])
        for i, sl in enumerate(slots):
            q_o[:, (2 * pr + i) * LANE:(2 * pr + i + 1) * LANE] = sl.astype(BF16)
    pkv = proj(C_KS, C_KC)
    for c0, aug_ref, out in ((C_KS, kas_ref, ks_o), (C_KW, kaw_ref, kw_o)):
        slots = head_slots(group(pkv, c0, C_KS), lambda i, mine, other: aug_ref[:, i * LANE:(i + 1) * LANE])
        for i, sl in enumerate(slots):
            out[:, i * LANE:(i + 1) * LANE] = sl.astype(BF16)
    for c0, out in ((C_VS, vs_o), (C_VW, vw_o)):
        for i, sl in enumerate(head_slots(group(pkv, c0, C_KS), lambda i, mine, other: other)):
            out[:, i * LANE:(i + 1) * LANE] = sl.astype(BF16)
    pmisc = proj(C_KC, C_QB)
    for c0, out in ((C_KC, kc_o), (C_VC, vc_o)):
        chunk_ref[...] = group(pmisc, c0, C_KC)
        for tk in range(CMP_STRIDE):
            out[:, tk * LANE:(tk + 1) * LANE] = chunk_ref[pl.ds(tk, TM_WIDE // CMP_STRIDE, stride=CMP_STRIDE), :].astype(BF16)
    gt_o[...] = jax.nn.sigmoid(group(pmisc, C_GT, C_KC) + gb_ref[...]).astype(BF16)
    z = _dot(group(pmisc, C_AL, C_KC).astype(BF16), wa2_ref[...]) + ba_ref[...]
    log_sig = jnp.minimum(z, 0.0) - jnp.log1p(jnp.exp(-jnp.abs(z)))
    la_o[...] = log_sig * (1.0 / GLA_TAU)
    qb_o[...] = proj(C_QB, C_KB).astype(BF16)
    kb_o[...] = proj(C_KB, C_VB).astype(BF16)
    vb_o[...] = proj(C_VB, C_RB).astype(BF16)
    rb_o[...] = proj(C_RB, C_MG).astype(BF16)


def _inproj(x2, g, wcat, qconst, kaug_s, kaug_w, gate_b, wa2, ba):
    n = x2.shape[0]
    tiles_per_seq = SEQ // TM_WIDE
    outs = [(1, NSA_HEADS * LANE, BF16), (1, LANE * 2, BF16), (1, LANE * 2, BF16), (1, LANE * 2, BF16),
            (1, LANE * 2, BF16), (CMP_STRIDE, CMP_STRIDE * LANE, BF16), (CMP_STRIDE, CMP_STRIDE * LANE, BF16),
            (1, LANE, BF16), (1, GLA_KEY_WIDTH, BF16), (1, GLA_KEY_WIDTH, BF16),
            (1, GLA_VAL_WIDTH, BF16), (1, GLA_VAL_WIDTH, BF16), (1, GLA_KEY_WIDTH, F32), (1, 2 * D_MODEL, BF16)]
    tile = lambda w, per=1: pl.BlockSpec((TM_WIDE // per, w), lambda i: (i, 0))
    return pl.pallas_call(
        _inproj_body,
        grid=(n // TM_WIDE,),
        in_specs=[tile(D_MODEL), _const_spec((1, D_MODEL)), _const_spec((D_MODEL, C_END)),
                  _const_spec((1, NSA_HEADS * LANE)),
                  pl.BlockSpec((TM_WIDE, 2 * LANE), lambda i: (i % tiles_per_seq, 0)),
                  pl.BlockSpec((TM_WIDE, 2 * LANE), lambda i: (i % tiles_per_seq, 0)),
                  _const_spec((1, LANE)), _const_spec((LANE, GLA_KEY_WIDTH)), _const_spec((1, GLA_KEY_WIDTH))],
        out_specs=[tile(w, per) for per, w, _ in outs],
        out_shape=[jax.ShapeDtypeStruct((n // per, w), dt) for per, w, dt in outs],
        scratch_shapes=[pltpu.VMEM((TM_WIDE, LANE), F32)],
        compiler_params=pltpu.CompilerParams(dimension_semantics=("arbitrary",), vmem_limit_bytes=VMEM_LIMIT),
        name="inproj",
    )(x2, g, wcat, qconst, kaug_s, kaug_w, gate_b, wa2, ba)


def _memkv_body(m_ref, g_ref, w_ref, k_o, v_o):
    hm = _rms(m_ref[0], g_ref[...]).astype(BF16)
    kv = _dot(hm, w_ref[...])
    k_o[0] = kv[:, :XATTN_WIDTH].astype(BF16)
    v_o[0] = kv[:, XATTN_WIDTH:].astype(BF16)


def _memkv(mem, g, w):
    b = mem.shape[0]
    blk = pl.BlockSpec((1, MEM_LEN, XATTN_WIDTH), lambda i: (i, 0, 0))
    return pl.pallas_call(
        _memkv_body,
        grid=(b,),
        in_specs=[pl.BlockSpec((1, MEM_LEN, D_MODEL), lambda i: (i, 0, 0)), _const_spec((1, D_MODEL)),
                  _const_spec((D_MODEL, 2 * XATTN_WIDTH))],
        out_specs=[blk, blk],
        out_shape=[jax.ShapeDtypeStruct((b, MEM_LEN, XATTN_WIDTH), BF16)] * 2,
        compiler_params=pltpu.CompilerParams(dimension_semantics=("arbitrary",)),
        name="memkv",
    )(mem, g, w)


def _compress(c_ref, wa_ref, wb_ref, w2_ref, p1_ref, p2_ref):
    c = c_ref[0].astype(F32)
    a = _dot((c + p1_ref[...]).astype(BF16), wa_ref[0])
    b = _dot((c + p2_ref[...]).astype(BF16), wb_ref[0])
    hid = a + pltpu.roll(b, N_CMP_PAD - 1, axis=0)
    return _dot(jax.nn.gelu(hid).astype(BF16), w2_ref[...]).astype(BF16)


def _nsa_constants(slope_ref, biasc_ref, wbias_ref, dbias_ref):
    t_col = lax.broadcasted_iota(jnp.int32, (SEQ, LANE), 0)
    n_row = lax.broadcasted_iota(jnp.int32, (SEQ, LANE), 1)
    dist_c = (t_col - (n_row * CMP_STRIDE + (CMP_BLOCK - 1))).astype(F32)
    for g in range(NSA_GROUP):
        biasc_ref[g] = jnp.where(dist_c >= 0.0, -slope_ref[0, g:g + 1, :] * dist_c, NEG_INF)
    col = lax.broadcasted_iota(jnp.int32, (QUAD_ROWS, WIN_KEYS), 1)
    row = lax.broadcasted_iota(jnp.int32, (QUAD_ROWS, WIN_KEYS), 0)
    dist_w = ((row >> 8) << 6) + (row & (SLC_BLOCK - 1)) + WIN_PAD - col
    wbias_ref[...] = jnp.where((dist_w >= 0) & (dist_w < WINDOW), 0.0, NEG_INF)
    col = lax.broadcasted_iota(jnp.int32, (QUAD_ROWS, QUAD_TOKENS), 1)
    row = lax.broadcasted_iota(jnp.int32, (QUAD_ROWS, QUAD_TOKENS), 0)
    blk_r, blk_c = row >> 8, col >> 6
    causal = (col & (SLC_BLOCK - 1)) <= (row & (SLC_BLOCK - 1))
    dbias_ref[...] = jnp.where((blk_c < blk_r) | ((blk_c == blk_r) & causal), 0.0, NEG_INF)


def _nsa_body(q_ref, ks_ref, kw_ref, vs_ref, vw_ref, kc_ref, vc_ref, gt_ref,
              wak_ref, wbk_ref, w2k_ref, wav_ref, wbv_ref, w2v_ref, p1k_ref, p2k_ref, p1v_ref, p2v_ref,
              rrep_ref, slope_ref, ovl_ref, o_ref,
              ocmp_ref, qs_ref, kwp_ref, vwp_ref, biasc_ref, wbias_ref, dbias_ref, s_ref, e_ref, sw_ref, ew_ref):
    @pl.when(pl.program_id(1) == 0)
    def _():
        _nsa_constants(slope_ref, biasc_ref, wbias_ref, dbias_ref)

    kcmp = _compress(kc_ref, wak_ref, wbk_ref, w2k_ref, p1k_ref, p2k_ref)
    vcmp = _compress(vc_ref, wav_ref, wbv_ref, w2v_ref, p1v_ref, p2v_ref)

    low_half = lax.broadcasted_iota(jnp.int32, (SEQ, LANE), 1) < HEAD_DIM
    has_key = lax.broadcasted_iota(jnp.int32, (SEQ, 1), 0) >= CMP_BLOCK - 1
    imp = jnp.zeros((SEQ, LANE), F32)
    ones_sq = jnp.ones((LANE, LANE), BF16)
    o_even = None
    for g in range(NSA_GROUP):
        s = _dot_nt(q_ref[0, :, g * LANE:(g + 1) * LANE], kcmp) + biasc_ref[g]
        e = jnp.exp2(s - jnp.max(s, axis=-1, keepdims=True))
        l = _dot(e.astype(BF16), ones_sq)
        p = e * jnp.where(has_key, 1.0 / l, 0.0)
        imp = imp + p
        oc = _dot(p.astype(BF16), vcmp)
        if g % 2 == 0:
            o_even = oc
        else:
            ocmp_ref[g // 2] = jnp.where(low_half, o_even, oc)

    hi, lo = _split_bf16(imp.T)
    ovl = ovl_ref[...]
    imp_b = _dot(ovl, hi) + _dot(ovl, lo)
    j = lax.broadcasted_iota(jnp.int32, (N_SLC, SEQ), 0)
    cur = lax.broadcasted_iota(jnp.int32, (N_SLC, SEQ), 1) >> 6
    forced = (j == 0) | (j == cur) | (j == cur - 1)
    val = jnp.where(forced, float(NSA_GROUP + 1), jnp.where(j > cur, -1.0, imp_b))
    j8 = lax.broadcasted_iota(jnp.int32, (8, SEQ), 0)
    vals = [val[8 * r:8 * r + 8, :] for r in range(N_SLC // 8)]
    ranks = [jnp.zeros((8, SEQ), F32) for _ in vals]
    for jp in range(N_SLC):
        vj = val[jp:jp + 1, :]
        for r, vr in enumerate(vals):
            if 8 * r + 7 <= jp:
                beats = vj > vr
            elif 8 * r > jp:
                beats = vj >= vr
            else:
                beats = (vj > vr) | ((vj == vr) & (j8 > jp - 8 * r))
            ranks[r] = ranks[r] + jnp.where(beats, 1.0, 0.0)
    rank = jnp.concatenate(ranks, axis=0)
    sel_bias = jnp.where((rank < float(N_SELECT)) & (j <= cur), 0.0, NEG_INF)
    aug_t = jnp.concatenate([jnp.zeros((AUG_SEL, SEQ), F32), sel_bias,
                             jnp.zeros((LANE - AUG_SEL - N_SLC, SEQ), F32)], axis=0)
    aug = aug_t.T.astype(BF16)
    for g in range(NSA_GROUP):
        qs_ref[g] = q_ref[0, :, g * LANE:(g + 1) * LANE] + aug

    pad_lane = lax.broadcasted_iota(jnp.int32, (WIN_PAD, LANE), 1)
    kwp_ref[0:WIN_PAD, :] = jnp.where(pad_lane == AUG_HI, NEG_INF, 0.0).astype(BF16)
    vwp_ref[0:WIN_PAD, :] = jnp.zeros((WIN_PAD, LANE), BF16)
    kwp_ref[WIN_PAD:, :] = kw_ref[0]
    vwp_ref[WIN_PAD:, :] = vw_ref[0]

    lane64 = lax.broadcasted_iota(jnp.int32, (SLC_BLOCK, LANE), 1) < HEAD_DIM

    def softmax_pv(s_buf, e_buf, rows, nkeys, v):
        m = jnp.max(s_buf[rows, 0:nkeys], axis=-1, keepdims=True)
        e = jnp.exp2(s_buf[rows, 0:nkeys] - m)
        e_buf[rows, 0:nkeys] = e.astype(BF16)
        return _dot(e_buf[rows, 0:nkeys], v) * (1.0 / jnp.sum(e, axis=-1, keepdims=True))

    def quad(iq):
        r0 = iq * QUAD_TOKENS
        nkeys = r0 + QUAD_TOKENS
        qs = jnp.concatenate([qs_ref[g, r0 + bq * SLC_BLOCK:r0 + (bq + 1) * SLC_BLOCK, :]
                              for bq in range(QUAD_TOKENS // SLC_BLOCK) for g in range(NSA_GROUP)], axis=0)
        halves = [slice(hf * QUAD_ROWS // 2, (hf + 1) * QUAD_ROWS // 2) for hf in range(2)]
        for rows in halves:
            for c0 in range(0, r0, KEY_CHUNK):
                c1 = min(c0 + KEY_CHUNK, r0)
                s_ref[rows, c0:c1] = _dot_nt(qs[rows], ks_ref[0, c0:c1, :])
            s_ref[rows, r0:nkeys] = _dot_nt(qs[rows], ks_ref[0, r0:nkeys, :]) + dbias_ref[rows, :]
        sw_ref[...] = _dot_nt(qs, kwp_ref[r0:r0 + WIN_KEYS, :]) + wbias_ref[...]
        o_s = jnp.concatenate([softmax_pv(s_ref, e_ref, rows, nkeys, vs_ref[0, 0:nkeys, :]) for rows in halves],
                              axis=0)
        o_w = softmax_pv(sw_ref, ew_ref, slice(None), WIN_KEYS, vwp_ref[r0:r0 + WIN_KEYS, :])
        gt = _dot(gt_ref[0, r0:r0 + QUAD_TOKENS, :], rrep_ref[0])
        for bq in range(QUAD_TOKENS // SLC_BLOCK):
            tok = slice(r0 + bq * SLC_BLOCK, r0 + (bq + 1) * SLC_BLOCK)
            gtb = gt[bq * SLC_BLOCK:(bq + 1) * SLC_BLOCK]
            for pp in range(2):
                a0 = (bq * NSA_GROUP + 2 * pp) * SLC_BLOCK
                a1, a2 = a0 + SLC_BLOCK, a0 + 2 * SLC_BLOCK
                osl = jnp.where(lane64, o_s[a0:a1], o_s[a1:a2])
                owi = jnp.where(lane64, o_w[a0:a1], o_w[a1:a2])
                out = (gtb[:, pp * LANE:(pp + 1) * LANE] * ocmp_ref[pp, tok, :]
                       + gtb[:, (2 + pp) * LANE:(3 + pp) * LANE] * osl
                       + gtb[:, (4 + pp) * LANE:(5 + pp) * LANE] * owi)
                o_ref[0, tok, pp * LANE:(pp + 1) * LANE] = out.astype(BF16)

    one = jnp.minimum(pl.program_id(1) + 1, 1)
    for iq in range(SEQ // QUAD_TOKENS):
        def body(_, carry, iq=iq):
            quad(iq)
            return carry

        lax.fori_loop(0, one, body, 0)


def _nsa(q, ks, kw, vs, vw, kc2, vc2, gt, wak, wbk, w2k, wav, wbv, w2v, p1k, p2k, p1v, p2v, rrep, slopes, ovl):
    b = q.shape[0]
    hs = lambda w: pl.BlockSpec((1, SEQ, w), lambda h, i: (i, 0, h))
    perh = lambda s: pl.BlockSpec((1,) + s, lambda h, i: (h,) + (0,) * len(s))
    cblk = pl.BlockSpec((1, N_CMP_PAD, CMP_STRIDE * 2 * HEAD_DIM), lambda h, i: (i, 0, 0))
    w1 = (CMP_STRIDE * 2 * HEAD_DIM, CMP_HIDDEN)
    return pl.pallas_call(
        _nsa_body,
        grid=(NSA_KV_HEADS, b),
        in_specs=[hs(NSA_GROUP * LANE), hs(LANE), hs(LANE), hs(LANE), hs(LANE), cblk, cblk,
                  pl.BlockSpec((1, SEQ, LANE), lambda h, i: (i, 0, 0)),
                  perh(w1), perh(w1), _const_spec((CMP_HIDDEN, LANE)),
                  perh(w1), perh(w1), _const_spec((CMP_HIDDEN, LANE)),
                  _const_spec((1, w1[0])), _const_spec((1, w1[0])), _const_spec((1, w1[0])), _const_spec((1, w1[0])),
                  perh((LANE, 6 * LANE)), perh((8, LANE)), _const_spec((N_SLC, LANE))],
        out_specs=pl.BlockSpec((1, SEQ, 2 * LANE), lambda h, i: (i, 0, h)),
        out_shape=jax.ShapeDtypeStruct((b, SEQ, NSA_HEADS * HEAD_DIM), BF16),
        scratch_shapes=[pltpu.VMEM((2, SEQ, LANE), F32), pltpu.VMEM((NSA_GROUP, SEQ, LANE), BF16),
                        pltpu.VMEM((SEQ + WIN_PAD, LANE), BF16), pltpu.VMEM((SEQ + WIN_PAD, LANE), BF16),
                        pltpu.VMEM((NSA_GROUP, SEQ, LANE), F32), pltpu.VMEM((QUAD_ROWS, WIN_KEYS), F32),
                        pltpu.VMEM((QUAD_ROWS, QUAD_TOKENS), F32),
                        pltpu.VMEM((QUAD_ROWS, SEQ), F32), pltpu.VMEM((QUAD_ROWS, SEQ), BF16),
                        pltpu.VMEM((QUAD_ROWS, WIN_KEYS), F32), pltpu.VMEM((QUAD_ROWS, WIN_KEYS), BF16)],
        compiler_params=pltpu.CompilerParams(dimension_semantics=("arbitrary", "arbitrary"),
                                             vmem_limit_bytes=VMEM_LIMIT),
        name="nsa",
    )(q, ks, kw, vs, vw, kc2, vc2, gt, wak, wbk, w2k, wav, wbv, w2v, p1k, p2k, p1v, p2v, rrep, slopes, ovl)


def _gla_body(q_ref, k_ref, v_ref, r_ref, la_ref, ng_ref, o_ref, oi_ref, qd_ref, inc_ref, dec_ref, st_ref):
    n_chunks = SEQ // GLA_CHUNK
    per_group = GLA_GROUP // GLA_CHUNK
    ri = lax.broadcasted_iota(jnp.int32, (GLA_GROUP, GLA_GROUP), 0)
    ci = lax.broadcasted_iota(jnp.int32, (GLA_GROUP, GLA_GROUP), 1)
    same_chunk = (ri >> 6) == (ci >> 6)
    tri = same_chunk & (ci <= ri)
    t_mat = jnp.where(tri, 1.0, 0.0).astype(BF16)
    o_mat = jnp.where(same_chunk, 1.0, 0.0).astype(BF16)
    head_g = lax.broadcasted_iota(jnp.int32, (GLA_GROUP, GLA_KEY_WIDTH), 1) >> 6
    head_c = lax.broadcasted_iota(jnp.int32, (GLA_CHUNK, GLA_KEY_WIDTH), 1) >> 6
    head_s = lax.broadcasted_iota(jnp.int32, (GLA_DV, GLA_KEY_WIDTH), 1) >> 6

    def group(gi):
        rows = slice(gi * GLA_GROUP, (gi + 1) * GLA_GROUP)
        hi, lo = _split_bf16(la_ref[0, rows, :])
        b = _dot(t_mat, hi) + _dot(t_mat, lo)
        b_last = _dot(o_mat, hi) + _dot(o_mat, lo)
        k = k_ref[0, rows, :].astype(F32)
        q_d = q_ref[0, rows, :].astype(F32) * jnp.exp(b)
        k_d = (k * jnp.exp(-b)).astype(BF16)
        k_s = (k * jnp.exp(b_last - b)).astype(BF16)
        qd_ref[rows, :] = q_d.astype(BF16)
        v = v_ref[0, rows, :]
        for h in range(GLA_HEADS):
            qm = jnp.where(head_g == h, q_d, 0.0).astype(BF16)
            att = jnp.where(tri, _dot_nt(qm, k_d), 0.0)
            oi_ref[rows, h * GLA_DV:(h + 1) * GLA_DV] = _dot(att.astype(BF16), v[:, h * GLA_DV:(h + 1) * GLA_DV])
        decay = jnp.exp(b_last)
        for c in range(per_group):
            cr = slice(c * GLA_CHUNK, (c + 1) * GLA_CHUNK)
            inc_full = _dot_tn(v[cr, :], k_s[cr, :])
            inc = jnp.zeros((GLA_DV, GLA_KEY_WIDTH), F32)
            for h in range(GLA_HEADS):
                inc = inc + jnp.where(head_s == h, inc_full[h * GLA_DV:(h + 1) * GLA_DV, :], 0.0)
            inc_ref[gi * per_group + c] = inc
            dec_ref[gi * per_group + c] = decay[c * GLA_CHUNK:c * GLA_CHUNK + 8, :]

    def emit(gi):
        for c in range(per_group):
            rows = slice(gi * GLA_GROUP + c * GLA_CHUNK, gi * GLA_GROUP + (c + 1) * GLA_CHUNK)
            q_d = qd_ref[rows, :]
            st = st_ref[gi * per_group + c]
            for h in range(GLA_HEADS):
                cols = slice(h * GLA_DV, (h + 1) * GLA_DV)
                qm = jnp.where(head_c == h, q_d, jnp.zeros_like(q_d))
                o = oi_ref[rows, cols] + _dot_nt(qm, st)
                o = o * lax.rsqrt(jnp.mean(o * o, axis=-1, keepdims=True) + RMS_EPS) * ng_ref[...]
                r = r_ref[0, rows, cols].astype(F32)
                o_ref[0, rows, cols] = (o * (r * jax.nn.sigmoid(r))).astype(BF16)

    state = jnp.zeros((GLA_DV, GLA_KEY_WIDTH), F32)
    for gi in range(SEQ // GLA_GROUP):
        group(gi)
        for n in range(gi * per_group, (gi + 1) * per_group):
            st_ref[n] = state.astype(BF16)
            state = state * dec_ref[n, 0:1, :] + inc_ref[n]
        emit(gi)


def _gla(qb, kb, vb, rb, la, ng):
    b = qb.shape[0]
    n_chunks = SEQ // GLA_CHUNK
    blk = lambda w: pl.BlockSpec((1, SEQ, w), lambda i: (i, 0, 0))
    return pl.pallas_call(
        _gla_body,
        grid=(b,),
        in_specs=[blk(GLA_KEY_WIDTH), blk(GLA_KEY_WIDTH), blk(GLA_VAL_WIDTH), blk(GLA_VAL_WIDTH),
                  blk(GLA_KEY_WIDTH), _const_spec((1, GLA_DV))],
        out_specs=blk(GLA_VAL_WIDTH),
        out_shape=jax.ShapeDtypeStruct((b, SEQ, GLA_VAL_WIDTH), BF16),
        scratch_shapes=[pltpu.VMEM((SEQ, GLA_VAL_WIDTH), F32), pltpu.VMEM((SEQ, GLA_KEY_WIDTH), BF16),
                        pltpu.VMEM((n_chunks, GLA_DV, GLA_KEY_WIDTH), F32),
                        pltpu.VMEM((n_chunks, 8, GLA_KEY_WIDTH), F32),
                        pltpu.VMEM((n_chunks, GLA_DV, GLA_KEY_WIDTH), BF16)],
        compiler_params=pltpu.CompilerParams(dimension_semantics=("arbitrary",), vmem_limit_bytes=VMEM_LIMIT),
        name="gla",
    )(qb, kb, vb, rb, la, ng)


def _mixx_body(x_ref, on_ref, og_ref, gm_ref, wn_ref, wg_ref, wo_ref, lx_ref, wq_ref, km_ref, vm_ref, wxo_ref,
               o_ref):
    gm = gm_ref[...].astype(F32)
    mix = (gm[:, :D_MODEL] * _dot(on_ref[...], wn_ref[...])
           + gm[:, D_MODEL:] * _dot(og_ref[...], wg_ref[...]))
    x1 = x_ref[...] + _dot(mix.astype(BF16), wo_ref[...])
    hx = _rms(x1, lx_ref[...]).astype(BF16)
    q = (_dot(hx, wq_ref[...]) * (XATTN_HEAD_DIM ** -0.5)).astype(BF16)
    outs = []
    for h in range(XATTN_HEADS):
        sl = slice(h * XATTN_HEAD_DIM, (h + 1) * XATTN_HEAD_DIM)
        s = _dot_nt(q[:, sl], km_ref[0, :, sl])
        e = jnp.exp(s - jnp.max(s, axis=-1, keepdims=True))
        l = jnp.sum(e, axis=-1, keepdims=True)
        outs.append((_dot(e.astype(BF16), vm_ref[0, :, sl]) / l).astype(BF16))
    o = jnp.concatenate(outs, axis=-1)
    o_ref[...] = x1 + _dot(o, wxo_ref[...])


def _mixx(x2, o_nsa, o_gla, gm, wn, wg, wo, lx, wq, km, vm, wxo):
    n = x2.shape[0]
    tiles_per_seq = SEQ // TM_WIDE
    tile = lambda w: pl.BlockSpec((TM_WIDE, w), lambda i: (i, 0))
    mblk = pl.BlockSpec((1, MEM_LEN, XATTN_WIDTH), lambda i: (i // tiles_per_seq, 0, 0))
    return pl.pallas_call(
        _mixx_body,
        grid=(n // TM_WIDE,),
        in_specs=[tile(D_MODEL), tile(XATTN_WIDTH), tile(GLA_VAL_WIDTH), tile(2 * D_MODEL),
                  _const_spec((XATTN_WIDTH, D_MODEL)), _const_spec((GLA_VAL_WIDTH, D_MODEL)),
                  _const_spec((D_MODEL, D_MODEL)), _const_spec((1, D_MODEL)),
                  _const_spec((D_MODEL, XATTN_WIDTH)), mblk, mblk, _const_spec((XATTN_WIDTH, D_MODEL))],
        out_specs=tile(D_MODEL),
        out_shape=jax.ShapeDtypeStruct((n, D_MODEL), F32),
        compiler_params=pltpu.CompilerParams(dimension_semantics=("arbitrary",), vmem_limit_bytes=VMEM_LIMIT),
        name="mixx",
    )(x2, o_nsa, o_gla, gm, wn, wg, wo, lx, wq, km, vm, wxo)


def _ffn_body(x_ref, lf_ref, wu_ref, wgt_ref, cw_ref, cb_ref, wd_ref, lfin_ref, o_ref, ubuf_ref, carry_ref):
    @pl.when(pl.program_id(0) % (SEQ // TM) == 0)
    def _():
        carry_ref[...] = jnp.zeros_like(carry_ref)

    x = x_ref[...]
    hf = _rms(x, lf_ref[...]).astype(BF16)
    acc = jnp.zeros((TM, D_MODEL), F32)
    for c0 in range(0, FFN_DIM, FFN_CHUNK):
        cols = slice(c0, min(c0 + FFN_CHUNK, FFN_DIM))
        w = cols.stop - cols.start
        ubuf_ref[0:8, 0:w] = carry_ref[:, cols]
        ubuf_ref[8:, 0:w] = _dot(hf, wu_ref[:, cols])
        carry_ref[:, cols] = ubuf_ref[TM:TM + 8, 0:w]
        u = (cw_ref[0:1, cols] * ubuf_ref[6:6 + TM, 0:w] + cw_ref[1:2, cols] * ubuf_ref[7:7 + TM, 0:w]
             + cw_ref[2:3, cols] * ubuf_ref[8:8 + TM, 0:w] + cb_ref[:, cols])
        act = (jax.nn.gelu(u) * _dot(hf, wgt_ref[:, cols])).astype(BF16)
        acc = acc + _dot(act, wd_ref[cols, :])
    o_ref[...] = _rms(x + acc, lfin_ref[...])


def _ffn(x2, lf, wu, wgt, cw, cb, wd, lfin):
    n = x2.shape[0]
    tile = pl.BlockSpec((TM, D_MODEL), lambda i: (i, 0))
    return pl.pallas_call(
        _ffn_body,
        grid=(n // TM,),
        in_specs=[tile, _const_spec((1, D_MODEL)), _const_spec((D_MODEL, FFN_DIM)), _const_spec((D_MODEL, FFN_DIM)),
                  _const_spec((CONV_WIDTH, FFN_DIM)), _const_spec((1, FFN_DIM)), _const_spec((FFN_DIM, D_MODEL)),
                  _const_spec((1, D_MODEL))],
        out_specs=tile,
        out_shape=jax.ShapeDtypeStruct((n, D_MODEL), F32),
        scratch_shapes=[pltpu.VMEM((TM + 8, FFN_CHUNK), F32), pltpu.VMEM((8, FFN_DIM), F32)],
        compiler_params=pltpu.CompilerParams(dimension_semantics=("arbitrary",), vmem_limit_bytes=VMEM_LIMIT),
        name="ffn",
    )(x2, lf, wu, wgt, cw, cb, wd, lfin)


def _layout_inproj(w_in):
    o = 0
    seg = {}
    for name, wdt in (("q", 512), ("kc", 128), ("vc", 128), ("ks", 128), ("vs", 128), ("kw", 128), ("vw", 128),
                      ("gt", 24), ("qb", 256), ("kb", 256), ("vb", 512), ("rb", 512), ("al", 16), ("mg", 2048)):
        seg[name] = w_in[:, o:o + wdt]
        o += wdt
    q_scale = (HEAD_DIM ** -0.5) * LOG2E
    cols = [seg["q"] * q_scale, seg["ks"], seg["kw"], seg["vs"], seg["vw"], seg["kc"], seg["vc"]]
    cols += [jnp.pad(seg["gt"], ((0, 0), (0, LANE - 3 * NSA_HEADS)))]
    cols += [jnp.pad(seg["al"], ((0, 0), (0, LANE - GLA_RANK)))]
    cols += [seg["qb"] * (GLA_DK ** -0.5), seg["kb"], seg["vb"], seg["rb"]]
    cols += [seg["mg"]]
    return jnp.concatenate(cols, axis=1).astype(BF16)


def _position_constants():
    slopes = 2.0 ** (-np.arange(1, NSA_HEADS + 1, dtype=np.float64)) * LOG2E
    qconst = np.zeros((1, NSA_HEADS * LANE), np.float32)
    for hh in range(NSA_HEADS):
        qconst[0, hh * LANE + AUG_HI] = slopes[hh]
        qconst[0, hh * LANE + AUG_LO] = slopes[hh]
    t = np.arange(SEQ)
    kaug_s = np.zeros((SEQ, 2 * LANE), np.float32)
    kaug_w = np.zeros((SEQ, 2 * LANE), np.float32)
    for h in range(NSA_KV_HEADS):
        kaug_s[t, h * LANE + AUG_SEL + t // SLC_BLOCK] = 1.0
        for a in (kaug_s, kaug_w):
            a[:, h * LANE + AUG_HI] = (t // SLC_BLOCK) * SLC_BLOCK
            a[:, h * LANE + AUG_LO] = t % SLC_BLOCK
    slope_rows = np.zeros((NSA_KV_HEADS, 8, LANE), np.float32)
    rrep = np.zeros((NSA_KV_HEADS, LANE, 6 * LANE), np.float32)
    for h in range(NSA_KV_HEADS):
        for g in range(NSA_GROUP):
            slope_rows[h, g, :] = slopes[h * NSA_GROUP + g]
            for c in range(3):
                pp, half = g // 2, g % 2
                c0 = (c * 2 + pp) * LANE + half * HEAD_DIM
                rrep[h, (h * NSA_GROUP + g) * 3 + c, c0:c0 + HEAD_DIM] = 1.0
    ovl = np.zeros((N_SLC, N_CMP_PAD), np.float32)
    for n in range(N_CMP_PAD - 1):
        for tok in range(n * CMP_STRIDE, n * CMP_STRIDE + CMP_BLOCK):
            ovl[tok // SLC_BLOCK, n] += 1.0 / CMP_BLOCK
    return qconst, kaug_s, kaug_w, slope_rows, rrep, ovl


def _layout_compress(w1, w2, pos, dup):
    w1 = w1.reshape(2, CMP_STRIDE, 1, HEAD_DIM, CMP_HIDDEN)
    sel = jnp.eye(NSA_KV_HEADS, dtype=F32)[:, None, None, :, None, None]
    wh = (w1[None] * sel).reshape(NSA_KV_HEADS, 2, CMP_STRIDE * NSA_KV_HEADS * HEAD_DIM, CMP_HIDDEN)
    p = jnp.broadcast_to(pos.reshape(2, CMP_STRIDE, 1, HEAD_DIM), (2, CMP_STRIDE, NSA_KV_HEADS, HEAD_DIM))
    p = p.reshape(2, 1, CMP_STRIDE * NSA_KV_HEADS * HEAD_DIM)
    w2p = jnp.concatenate([w2, w2 if dup else jnp.zeros_like(w2)], axis=1)
    return wh[:, 0].astype(BF16), wh[:, 1].astype(BF16), w2p.astype(BF16), p[0], p[1]


def kernel(x, mem, ln_mix_g, w_in, nsa_gate_b, cmp_pos_k, cmp_w1_k, cmp_w2_k, cmp_pos_v, cmp_w1_v, cmp_w2_v,
           gla_w_alpha2, gla_b_alpha, gla_norm_g, w_branch_nsa, w_branch_gla, w_out, ln_x_g, ln_mem_g, w_xq,
           w_xkv, w_xo, ln_ffn_g, w_up, conv_w, conv_b, w_down, ln_final_g):
    b = x.shape[0]
    n = b * SEQ
    row = lambda v: v.reshape(1, -1).astype(F32)
    qconst, kaug_s, kaug_w, slope_rows, rrep, ovl = _position_constants()

    x2 = x.reshape(n, D_MODEL)
    gate_b = jnp.pad(nsa_gate_b[0], (0, LANE - 3 * NSA_HEADS)).reshape(1, LANE)
    wa2 = jnp.pad(gla_w_alpha2[0], ((0, LANE - GLA_RANK), (0, 0))).astype(BF16)
    (q, ks, kw, vs, vw, kc, vc, gt, qb, kb, vb, rb, la, gm) = _inproj(
        x2, row(ln_mix_g[0]), _layout_inproj(w_in[0]), qconst, kaug_s, kaug_w, gate_b, wa2, row(gla_b_alpha[0]))

    s3 = lambda a: a.reshape(b, SEQ, a.shape[-1])
    chunked = lambda a: a.reshape(b, N_CMP_PAD, a.shape[-1])
    wak, wbk, w2k, p1k, p2k = _layout_compress(cmp_w1_k[0], cmp_w2_k[0], cmp_pos_k[0], dup=False)
    wav, wbv, w2v, p1v, p2v = _layout_compress(cmp_w1_v[0], cmp_w2_v[0], cmp_pos_v[0], dup=True)
    o_nsa = _nsa(s3(q), s3(ks), s3(kw), s3(vs), s3(vw), chunked(kc), chunked(vc), s3(gt),
                 wak, wbk, w2k, wav, wbv, w2v, p1k, p2k, p1v, p2v,
                 jnp.asarray(rrep, BF16), jnp.asarray(slope_rows), jnp.asarray(ovl, BF16))

    o_gla = _gla(s3(qb), s3(kb), s3(vb), s3(rb), s3(la), row(gla_norm_g[0]))

    km, vm = _memkv(mem, row(ln_mem_g[0]), w_xkv[0].astype(BF16))
    x_mid = _mixx(x2, o_nsa.reshape(n, -1), o_gla.reshape(n, -1), gm,
                  w_branch_nsa[0].astype(BF16), w_branch_gla[0].astype(BF16), w_out[0].astype(BF16),
                  row(ln_x_g[0]), w_xq[0].astype(BF16), km, vm, w_xo[0].astype(BF16))

    out = _ffn(x_mid, row(ln_ffn_g[0]), w_up[0][:, :FFN_DIM].astype(BF16), w_up[0][:, FFN_DIM:].astype(BF16),
               conv_w[0], conv_b[0].reshape(1, FFN_DIM), w_down[0].astype(BF16), row(ln_final_g))
    return out.reshape(b, SEQ, D_MODEL)
```

```python
import math

import jax
import jax.numpy as jnp
import numpy as np
from jax import lax
from jax.experimental import pallas as pl
from jax.experimental.pallas import tpu as pltpu

F32 = jnp.float32
BF16 = jnp.bfloat16

D_MODEL = 1024
SEQ = 2048
MEM_LEN = 256
HEAD_DIM = 64
NSA_HEADS = 8
NSA_KV_HEADS = 2
NSA_GROUP = 4
CMP_BLOCK = 32
CMP_STRIDE = 16
CMP_HIDDEN = 128
N_CMP_PAD = SEQ // CMP_STRIDE
SLC_BLOCK = 64
N_SLC = SEQ // SLC_BLOCK
N_SELECT = 8
WINDOW = 256
GLA_HEADS = 4
GLA_DK = 64
GLA_DV = 128
GLA_KEY_WIDTH = 256
GLA_VAL_WIDTH = 512
GLA_RANK = 16
GLA_TAU = 16.0
GLA_CHUNK = 64
GLA_GROUP = 256
XATTN_HEADS = 4
XATTN_HEAD_DIM = 128
XATTN_WIDTH = 512
FFN_DIM = 2816
CONV_WIDTH = 3
RMS_EPS = 1e-6
NEG_INF = -1e30
LOG2E = math.log2(math.e)

LANE = 128
VMEM_LIMIT = 56 * 1024 * 1024

AUG_SEL = HEAD_DIM
AUG_HI = HEAD_DIM + N_SLC
AUG_LO = AUG_HI + 1

C_Q = 0
C_KS = C_Q + NSA_HEADS * HEAD_DIM
C_KW = C_KS + LANE
C_VS = C_KW + LANE
C_VW = C_VS + LANE
C_KC = C_VW + LANE
C_VC = C_KC + LANE
C_GT = C_VC + LANE
C_AL = C_GT + LANE
C_QB = C_AL + LANE
C_KB = C_QB + GLA_KEY_WIDTH
C_VB = C_KB + GLA_KEY_WIDTH
C_RB = C_VB + GLA_VAL_WIDTH
C_MG = C_RB + GLA_VAL_WIDTH
C_END = C_MG + 2 * D_MODEL

TM = 512
TM_WIDE = 1024
QUAD_TOKENS = 4 * SLC_BLOCK
QUAD_ROWS = NSA_GROUP * QUAD_TOKENS
WIN_PAD = WINDOW
WIN_KEYS = WIN_PAD + QUAD_TOKENS
KEY_CHUNK = 512
MXU_DIM = 256
FFN_CHUNK = 11 * MXU_DIM


def _rms(x, g):
    return x * lax.rsqrt(jnp.mean(x * x, axis=-1, keepdims=True) + RMS_EPS) * g


def _dot(a, b, **kw):
    return jnp.dot(a, b, preferred_element_type=F32, **kw)


def _dot_nt(a, b):
    return lax.dot_general(a, b, (((1,), (1,)), ((), ())), preferred_element_type=F32)


def _dot_tn(a, b):
    return lax.dot_general(a, b, (((0,), (0,)), ((), ())), preferred_element_type=F32)


def _split_bf16(x):
    hi = x.astype(BF16)
    return hi, (x - hi.astype(F32)).astype(BF16)


def _const_spec(shape):
    return pl.BlockSpec(shape, lambda *_: (0,) * len(shape), pipeline_mode=pl.Buffered(1))


def _inproj_body(x_ref, g_ref, w_ref, qc_ref, kas_ref, kaw_ref, gb_ref, wa2_ref, ba_ref,
                 q_o, ks_o, kw_o, vs_o, vw_o, kc_o, vc_o, gt_o, qb_o, kb_o, vb_o, rb_o, la_o, gm_o, chunk_ref):
    h = _rms(x_ref[...], g_ref[...]).astype(BF16)

    def proj(lo, hi):
        return _dot(h, w_ref[:, lo:hi])

    gm_o[...] = jax.nn.sigmoid(proj(C_MG, C_END)).astype(BF16)

    low = lax.broadcasted_iota(jnp.int32, (TM_WIDE, LANE), 1) < HEAD_DIM

    def head_slots(x, fill):
        swapped = pltpu.roll(x, HEAD_DIM, axis=1)
        return [jnp.where(low, x, fill(0, x, swapped)), jnp.where(low, swapped, fill(1, swapped, x))]

    def group(wide, c0, base):
        return wide[:, c0 - base:c0 - base + LANE]

    pq = proj(C_Q, C_KS)
    for pr in range(NSA_HEADS // 2):
        slots = head_slots(group(pq, C_Q + pr * LANE, C_Q),
                           lambda i, mine, other: qc_ref[:, (2 * pr + i) * LANE:(2 * pr + i + 1) * LANE])
        for i, sl in enumerate(slots):
            q_o[:, (2 * pr + i) * LANE:(2 * pr + i + 1) * LANE] = sl.astype(BF16)
    pkv = proj(C_KS, C_KC)
    for c0, aug_ref, out in ((C_KS, kas_ref, ks_o), (C_KW, kaw_ref, kw_o)):
        slots = head_slots(group(pkv, c0, C_KS), lambda i, mine, other: aug_ref[:, i * LANE:(i + 1) * LANE])
        for i, sl in enumerate(slots):
            out[:, i * LANE:(i + 1) * LANE] = sl.astype(BF16)
    for c0, out in ((C_VS, vs_o), (C_VW, vw_o)):
        for i, sl in enumerate(head_slots(group(pkv, c0, C_KS), lambda i, mine, other: other)):
            out[:, i * LANE:(i + 1) * LANE] = sl.astype(BF16)
    pmisc = proj(C_KC, C_QB)
    for c0, out in ((C_KC, kc_o), (C_VC, vc_o)):
        chunk_ref[...] = group(pmisc, c0, C_KC)
        for tk in range(CMP_STRIDE):
            out[:, tk * LANE:(tk + 1) * LANE] = chunk_ref[pl.ds(tk, TM_WIDE // CMP_STRIDE, stride=CMP_STRIDE), :].astype(BF16)
    gt_o[...] = jax.nn.sigmoid(group(pmisc, C_GT, C_KC) + gb_ref[...]).astype(BF16)
    z = _dot(group(pmisc, C_AL, C_KC).astype(BF16), wa2_ref[...]) + ba_ref[...]
    log_sig = jnp.minimum(z, 0.0) - jnp.log1p(jnp.exp(-jnp.abs(z)))
    la_o[...] = log_sig * (1.0 / GLA_TAU)
    qb_o[...] = proj(C_QB, C_KB).astype(BF16)
    kb_o[...] = proj(C_KB, C_VB).astype(BF16)
    vb_o[...] = proj(C_VB, C_RB).astype(BF16)
    rb_o[...] = proj(C_RB, C_MG).astype(BF16)


def _inproj(x2, g, wcat, qconst, kaug_s, kaug_w, gate_b, wa2, ba):
    n = x2.shape[0]
    tiles_per_seq = SEQ // TM_WIDE
    outs = [(1, NSA_HEADS * LANE, BF16), (1, LANE * 2, BF16), (1, LANE * 2, BF16), (1, LANE * 2, BF16),
            (1, LANE * 2, BF16), (CMP_STRIDE, CMP_STRIDE * LANE, BF16), (CMP_STRIDE, CMP_STRIDE * LANE, BF16),
            (1, LANE, BF16), (1, GLA_KEY_WIDTH, BF16), (1, GLA_KEY_WIDTH, BF16),
            (1, GLA_VAL_WIDTH, BF16), (1, GLA_VAL_WIDTH, BF16), (1, GLA_KEY_WIDTH, F32), (1, 2 * D_MODEL, BF16)]
    tile = lambda w, per=1: pl.BlockSpec((TM_WIDE // per, w), lambda i: (i, 0))
    return pl.pallas_call(
        _inproj_body,
        grid=(n // TM_WIDE,),
        in_specs=[tile(D_MODEL), _const_spec((1, D_MODEL)), _const_spec((D_MODEL, C_END)),
                  _const_spec((1, NSA_HEADS * LANE)),
                  pl.BlockSpec((TM_WIDE, 2 * LANE), lambda i: (i % tiles_per_seq, 0)),
                  pl.BlockSpec((TM_WIDE, 2 * LANE), lambda i: (i % tiles_per_seq, 0)),
                  _const_spec((1, LANE)), _const_spec((LANE, GLA_KEY_WIDTH)), _const_spec((1, GLA_KEY_WIDTH))],
        out_specs=[tile(w, per) for per, w, _ in outs],
        out_shape=[jax.ShapeDtypeStruct((n // per, w), dt) for per, w, dt in outs],
        scratch_shapes=[pltpu.VMEM((TM_WIDE, LANE), F32)],
        compiler_params=pltpu.CompilerParams(dimension_semantics=("arbitrary",), vmem_limit_bytes=VMEM_LIMIT),
        name="inproj",
    )(x2, g, wcat, qconst, kaug_s, kaug_w, gate_b, wa2, ba)


def _memkv_body(m_ref, g_ref, w_ref, k_o, v_o):
    hm = _rms(m_ref[0], g_ref[...]).astype(BF16)
    kv = _dot(hm, w_ref[...])
    k_o[0] = kv[:, :XATTN_WIDTH].astype(BF16)
    v_o[0] = kv[:, XATTN_WIDTH:].astype(BF16)


def _memkv(mem, g, w):
    b = mem.shape[0]
    blk = pl.BlockSpec((1, MEM_LEN, XATTN_WIDTH), lambda i: (i, 0, 0))
    return pl.pallas_call(
        _memkv_body,
        grid=(b,),
        in_specs=[pl.BlockSpec((1, MEM_LEN, D_MODEL), lambda i: (i, 0, 0)), _const_spec((1, D_MODEL)),
                  _const_spec((D_MODEL, 2 * XATTN_WIDTH))],
        out_specs=[blk, blk],
        out_shape=[jax.ShapeDtypeStruct((b, MEM_LEN, XATTN_WIDTH), BF16)] * 2,
        compiler_params=pltpu.CompilerParams(dimension_semantics=("arbitrary",)),
        name="memkv",
    )(mem, g, w)


def _compress(c_ref, wa_ref, wb_ref, w2_ref, p1_ref, p2_ref):
    c = c_ref[0].astype(F32)
    a = _dot((c + p1_ref[...]).astype(BF16), wa_ref[0])
    b = _dot((c + p2_ref[...]).astype(BF16), wb_ref[0])
    hid = a + pltpu.roll(b, N_CMP_PAD - 1, axis=0)
    return _dot(jax.nn.gelu(hid).astype(BF16), w2_ref[...]).astype(BF16)


def _nsa_constants(slope_ref, biasc_ref, wbias_ref, dbias_ref):
    t_col = lax.broadcasted_iota(jnp.int32, (SEQ, LANE), 0)
    n_row = lax.broadcasted_iota(jnp.int32, (SEQ, LANE), 1)
    dist_c = (t_col - (n_row * CMP_STRIDE + (CMP_BLOCK - 1))).astype(F32)
    for g in range(NSA_GROUP):
        biasc_ref[g] = jnp.where(dist_c >= 0.0, -slope_ref[0, g:g + 1, :] * dist_c, NEG_INF)
    col = lax.broadcasted_iota(jnp.int32, (QUAD_ROWS, WIN_KEYS), 1)
    row = lax.broadcasted_iota(jnp.int32, (QUAD_ROWS, WIN_KEYS), 0)
    dist_w = ((row >> 8) << 6) + (row & (SLC_BLOCK - 1)) + WIN_PAD - col
    wbias_ref[...] = jnp.where((dist_w >= 0) & (dist_w < WINDOW), 0.0, NEG_INF)
    col = lax.broadcasted_iota(jnp.int32, (QUAD_ROWS, QUAD_TOKENS), 1)
    row = lax.broadcasted_iota(jnp.int32, (QUAD_ROWS, QUAD_TOKENS), 0)
    blk_r, blk_c = row >> 8, col >> 6
    causal = (col & (SLC_BLOCK - 1)) <= (row & (SLC_BLOCK - 1))
    dbias_ref[...] = jnp.where((blk_c < blk_r) | ((blk_c == blk_r) & causal), 0.0, NEG_INF)


def _nsa_body(q_ref, ks_ref, kw_ref, vs_ref, vw_ref, kc_ref, vc_ref, gt_ref,
              wak_ref, wbk_ref, w2k_ref, wav_ref, wbv_ref, w2v_ref, p1k_ref, p2k_ref, p1v_ref, p2v_ref,
              rrep_ref, slope_ref, ovl_ref, o_ref,
              ocmp_ref, qs_ref, biasc_ref, wbias_ref, dbias_ref, s_ref, e_ref, sw_ref, ew_ref):
    @pl.when(pl.program_id(1) == 0)
    def _():
        _nsa_constants(slope_ref, biasc_ref, wbias_ref, dbias_ref)

    kcmp = _compress(kc_ref, wak_ref, wbk_ref, w2k_ref, p1k_ref, p2k_ref)
    vcmp = _compress(vc_ref, wav_ref, wbv_ref, w2v_ref, p1v_ref, p2v_ref)

    low_half = lax.broadcasted_iota(jnp.int32, (SEQ, LANE), 1) < HEAD_DIM
    has_key = lax.broadcasted_iota(jnp.int32, (SEQ, 1), 0) >= CMP_BLOCK - 1
    imp = jnp.zeros((SEQ, LANE), F32)
    ones_sq = jnp.ones((LANE, LANE), BF16)
    o_even = None
    for g in range(NSA_GROUP):
        s = _dot_nt(q_ref[0, :, g * LANE:(g + 1) * LANE], kcmp) + biasc_ref[g]
        e = jnp.exp2(s - jnp.max(s, axis=-1, keepdims=True))
        l = _dot(e.astype(BF16), ones_sq)
        p = e * jnp.where(has_key, 1.0 / l, 0.0)
        imp = imp + p
        oc = _dot(p.astype(BF16), vcmp)
        if g % 2 == 0:
            o_even = oc
        else:
            ocmp_ref[g // 2] = jnp.where(low_half, o_even, oc)

    hi, lo = _split_bf16(imp.T)
    ovl = ovl_ref[...]
    imp_b = _dot(ovl, hi) + _dot(ovl, lo)
    j = lax.broadcasted_iota(jnp.int32, (N_SLC, SEQ), 0)
    cur = lax.broadcasted_iota(jnp.int32, (N_SLC, SEQ), 1) >> 6
    forced = (j == 0) | (j == cur) | (j == cur - 1)
    val = jnp.where(forced, float(NSA_GROUP + 1), jnp.where(j > cur, -1.0, imp_b))
    j8 = lax.broadcasted_iota(jnp.int32, (8, LANE), 0)
    rank_tiles = []
    for lt in range(SEQ // LANE):
        n_blk = (lt + 1) * LANE // SLC_BLOCK
        n_grp = -(-n_blk // 8)
        vt = val[:, lt * LANE:(lt + 1) * LANE]
        vals = [vt[8 * r:8 * r + 8, :] for r in range(n_grp)]
        ranks = [jnp.zeros((8, LANE), F32) for _ in vals]
        for jp in range(n_blk):
            vj = vt[jp:jp + 1, :]
            for r, vr in enumerate(vals):
                if 8 * r + 7 <= jp:
                    beats = vj > vr
                elif 8 * r > jp:
                    beats = vj >= vr
                else:
                    beats = (vj > vr) | ((vj == vr) & (j8 > jp - 8 * r))
                ranks[r] = ranks[r] + jnp.where(beats, 1.0, 0.0)
        ranks += [jnp.zeros((8, LANE), F32)] * (N_SLC // 8 - n_grp)
        rank_tiles.append(jnp.concatenate(ranks, axis=0))
    rank = jnp.concatenate(rank_tiles, axis=1)
    sel_bias = jnp.where((rank < float(N_SELECT)) & (j <= cur), 0.0, NEG_INF)
    aug_t = jnp.concatenate([jnp.zeros((AUG_SEL, SEQ), F32), sel_bias,
                             jnp.zeros((LANE - AUG_SEL - N_SLC, SEQ), F32)], axis=0)
    aug = aug_t.T.astype(BF16)
    for g in range(NSA_GROUP):
        qs_ref[g] = q_ref[0, :, g * LANE:(g + 1) * LANE] + aug

    lane64 = lax.broadcasted_iota(jnp.int32, (SLC_BLOCK, LANE), 1) < HEAD_DIM

    def softmax_pv(s_buf, e_buf, rows, nkeys, v):
        m = jnp.max(s_buf[rows, 0:nkeys], axis=-1, keepdims=True)
        e = jnp.exp2(s_buf[rows, 0:nkeys] - m)
        e_buf[rows, 0:nkeys] = e.astype(BF16)
        return _dot(e_buf[rows, 0:nkeys], v) * (1.0 / jnp.sum(e, axis=-1, keepdims=True))

    def quad(iq):
        r0 = iq * QUAD_TOKENS
        nkeys = r0 + QUAD_TOKENS
        qs = jnp.concatenate([qs_ref[g, r0 + bq * SLC_BLOCK:r0 + (bq + 1) * SLC_BLOCK, :]
                              for bq in range(QUAD_TOKENS // SLC_BLOCK) for g in range(NSA_GROUP)], axis=0)
        halves = [slice(hf * QUAD_ROWS // 2, (hf + 1) * QUAD_ROWS // 2) for hf in range(2)]
        for rows in halves:
            for c0 in range(0, r0, KEY_CHUNK):
                c1 = min(c0 + KEY_CHUNK, r0)
                s_ref[rows, c0:c1] = _dot_nt(qs[rows], ks_ref[0, c0:c1, :])
            s_ref[rows, r0:nkeys] = _dot_nt(qs[rows], ks_ref[0, r0:nkeys, :]) + dbias_ref[rows, :]
        w0 = max(r0 - WIN_PAD, 0)
        nwin = r0 + QUAD_TOKENS - w0
        sw_ref[:, 0:nwin] = _dot_nt(qs, kw_ref[0, w0:w0 + nwin, :]) + wbias_ref[:, WIN_KEYS - nwin:]
        o_s = jnp.concatenate([softmax_pv(s_ref, e_ref, rows, nkeys, vs_ref[0, 0:nkeys, :]) for rows in halves],
                              axis=0)
        o_w = softmax_pv(sw_ref, ew_ref, slice(None), nwin, vw_ref[0, w0:w0 + nwin, :])
        gt = _dot(gt_ref[0, r0:r0 + QUAD_TOKENS, :], rrep_ref[0])
        for bq in range(QUAD_TOKENS // SLC_BLOCK):
            tok = slice(r0 + bq * SLC_BLOCK, r0 + (bq + 1) * SLC_BLOCK)
            gtb = gt[bq * SLC_BLOCK:(bq + 1) * SLC_BLOCK]
            for pp in range(2):
                a0 = (bq * NSA_GROUP + 2 * pp) * SLC_BLOCK
                a1, a2 = a0 + SLC_BLOCK, a0 + 2 * SLC_BLOCK
                osl = jnp.where(lane64, o_s[a0:a1], o_s[a1:a2])
                owi = jnp.where(lane64, o_w[a0:a1], o_w[a1:a2])
                out = (gtb[:, pp * LANE:(pp + 1) * LANE] * ocmp_ref[pp, tok, :]
                       + gtb[:, (2 + pp) * LANE:(3 + pp) * LANE] * osl
                       + gtb[:, (4 + pp) * LANE:(5 + pp) * LANE] * owi)
                o_ref[0, tok, pp * LANE:(pp + 1) * LANE] = out.astype(BF16)

    one = jnp.minimum(pl.program_id(1) + 1, 1)
    for iq in range(SEQ // QUAD_TOKENS):
        def body(_, carry, iq=iq):
            quad(iq)
            return carry

        lax.fori_loop(0, one, body, 0)


def _nsa(q, ks, kw, vs, vw, kc2, vc2, gt, wak, wbk, w2k, wav, wbv, w2v, p1k, p2k, p1v, p2v, rrep, slopes, ovl):
    b = q.shape[0]
    hs = lambda w: pl.BlockSpec((1, SEQ, w), lambda h, i: (i, 0, h))
    perh = lambda s: pl.BlockSpec((1,) + s, lambda h, i: (h,) + (0,) * len(s))
    cblk = pl.BlockSpec((1, N_CMP_PAD, CMP_STRIDE * 2 * HEAD_DIM), lambda h, i: (i, 0, 0))
    w1 = (CMP_STRIDE * 2 * HEAD_DIM, CMP_HIDDEN)
    return pl.pallas_call(
        _nsa_body,
        grid=(NSA_KV_HEADS, b),
        in_specs=[hs(NSA_GROUP * LANE), hs(LANE), hs(LANE), hs(LANE), hs(LANE), cblk, cblk,
                  pl.BlockSpec((1, SEQ, LANE), lambda h, i: (i, 0, 0)),
                  perh(w1), perh(w1), _const_spec((CMP_HIDDEN, LANE)),
                  perh(w1), perh(w1), _const_spec((CMP_HIDDEN, LANE)),
                  _const_spec((1, w1[0])), _const_spec((1, w1[0])), _const_spec((1, w1[0])), _const_spec((1, w1[0])),
                  perh((LANE, 6 * LANE)), perh((8, LANE)), _const_spec((N_SLC, LANE))],
        out_specs=pl.BlockSpec((1, SEQ, 2 * LANE), lambda h, i: (i, 0, h)),
        out_shape=jax.ShapeDtypeStruct((b, SEQ, NSA_HEADS * HEAD_DIM), BF16),
        scratch_shapes=[pltpu.VMEM((2, SEQ, LANE), F32), pltpu.VMEM((NSA_GROUP, SEQ, LANE), BF16),
                        pltpu.VMEM((NSA_GROUP, SEQ, LANE), F32), pltpu.VMEM((QUAD_ROWS, WIN_KEYS), F32),
                        pltpu.VMEM((QUAD_ROWS, QUAD_TOKENS), F32),
                        pltpu.VMEM((QUAD_ROWS, SEQ), F32), pltpu.VMEM((QUAD_ROWS, SEQ), BF16),
                        pltpu.VMEM((QUAD_ROWS, WIN_KEYS), F32), pltpu.VMEM((QUAD_ROWS, WIN_KEYS), BF16)],
        compiler_params=pltpu.CompilerParams(dimension_semantics=("arbitrary", "arbitrary"),
                                             vmem_limit_bytes=VMEM_LIMIT),
        name="nsa",
    )(q, ks, kw, vs, vw, kc2, vc2, gt, wak, wbk, w2k, wav, wbv, w2v, p1k, p2k, p1v, p2v, rrep, slopes, ovl)


def _gla_body(q_ref, k_ref, v_ref, r_ref, la_ref, ng_ref, o_ref, oi_ref, qd_ref, inc_ref, dec_ref, st_ref):
    n_chunks = SEQ // GLA_CHUNK
    per_group = GLA_GROUP // GLA_CHUNK
    ri = lax.broadcasted_iota(jnp.int32, (GLA_GROUP, GLA_GROUP), 0)
    ci = lax.broadcasted_iota(jnp.int32, (GLA_GROUP, GLA_GROUP), 1)
    same_chunk = (ri >> 6) == (ci >> 6)
    tri = same_chunk & (ci <= ri)
    t_mat = jnp.where(tri, 1.0, 0.0).astype(BF16)
    o_mat = jnp.where(same_chunk, 1.0, 0.0).astype(BF16)
    head_g = lax.broadcasted_iota(jnp.int32, (GLA_GROUP, GLA_KEY_WIDTH), 1) >> 6
    head_c = lax.broadcasted_iota(jnp.int32, (GLA_CHUNK, GLA_KEY_WIDTH), 1) >> 6
    head_s = lax.broadcasted_iota(jnp.int32, (GLA_DV, GLA_KEY_WIDTH), 1) >> 6

    def group(gi):
        rows = slice(gi * GLA_GROUP, (gi + 1) * GLA_GROUP)
        hi, lo = _split_bf16(la_ref[0, rows, :])
        b = _dot(t_mat, hi) + _dot(t_mat, lo)
        b_last = _dot(o_mat, hi) + _dot(o_mat, lo)
        k = k_ref[0, rows, :].astype(F32)
        q_d = q_ref[0, rows, :].astype(F32) * jnp.exp(b)
        k_d = (k * jnp.exp(-b)).astype(BF16)
        k_s = (k * jnp.exp(b_last - b)).astype(BF16)
        qd_ref[rows, :] = q_d.astype(BF16)
        v = v_ref[0, rows, :]
        for h in range(GLA_HEADS):
            qm = jnp.where(head_g == h, q_d, 0.0).astype(BF16)
            att = jnp.where(tri, _dot_nt(qm, k_d), 0.0)
            oi_ref[rows, h * GLA_DV:(h + 1) * GLA_DV] = _dot(att.astype(BF16), v[:, h * GLA_DV:(h + 1) * GLA_DV])
        decay = jnp.exp(b_last)
        for c in range(per_group):
            cr = slice(c * GLA_CHUNK, (c + 1) * GLA_CHUNK)
            inc_full = _dot_tn(v[cr, :], k_s[cr, :])
            inc = jnp.zeros((GLA_DV, GLA_KEY_WIDTH), F32)
            for h in range(GLA_HEADS):
                inc = inc + jnp.where(head_s == h, inc_full[h * GLA_DV:(h + 1) * GLA_DV, :], 0.0)
            inc_ref[gi * per_group + c] = inc
            dec_ref[gi * per_group + c] = decay[c * GLA_CHUNK:c * GLA_CHUNK + 8, :]

    def emit(gi):
        for c in range(per_group):
            rows = slice(gi * GLA_GROUP + c * GLA_CHUNK, gi * GLA_GROUP + (c + 1) * GLA_CHUNK)
            q_d = qd_ref[rows, :]
            st = st_ref[gi * per_group + c]
            for h in range(GLA_HEADS):
                cols = slice(h * GLA_DV, (h + 1) * GLA_DV)
                qm = jnp.where(head_c == h, q_d, jnp.zeros_like(q_d))
                o = oi_ref[rows, cols] + _dot_nt(qm, st)
                o = o * lax.rsqrt(jnp.mean(o * o, axis=-1, keepdims=True) + RMS_EPS) * ng_ref[...]
                r = r_ref[0, rows, cols].astype(F32)
                o_ref[0, rows, cols] = (o * (r * jax.nn.sigmoid(r))).astype(BF16)

    state = jnp.zeros((GLA_DV, GLA_KEY_WIDTH), F32)
    for gi in range(SEQ // GLA_GROUP):
        group(gi)
        for n in range(gi * per_group, (gi + 1) * per_group):
            st_ref[n] = state.astype(BF16)
            state = state * dec_ref[n, 0:1, :] + inc_ref[n]
        emit(gi)


def _gla(qb, kb, vb, rb, la, ng):
    b = qb.shape[0]
    n_chunks = SEQ // GLA_CHUNK
    blk = lambda w: pl.BlockSpec((1, SEQ, w), lambda i: (i, 0, 0))
    return pl.pallas_call(
        _gla_body,
        grid=(b,),
        in_specs=[blk(GLA_KEY_WIDTH), blk(GLA_KEY_WIDTH), blk(GLA_VAL_WIDTH), blk(GLA_VAL_WIDTH),
                  blk(GLA_KEY_WIDTH), _const_spec((1, GLA_DV))],
        out_specs=blk(GLA_VAL_WIDTH),
        out_shape=jax.ShapeDtypeStruct((b, SEQ, GLA_VAL_WIDTH), BF16),
        scratch_shapes=[pltpu.VMEM((SEQ, GLA_VAL_WIDTH), F32), pltpu.VMEM((SEQ, GLA_KEY_WIDTH), BF16),
                        pltpu.VMEM((n_chunks, GLA_DV, GLA_KEY_WIDTH), F32),
                        pltpu.VMEM((n_chunks, 8, GLA_KEY_WIDTH), F32),
                        pltpu.VMEM((n_chunks, GLA_DV, GLA_KEY_WIDTH), BF16)],
        compiler_params=pltpu.CompilerParams(dimension_semantics=("arbitrary",), vmem_limit_bytes=VMEM_LIMIT),
        name="gla",
    )(qb, kb, vb, rb, la, ng)


def _mixx_body(x_ref, on_ref, og_ref, gm_ref, wn_ref, wg_ref, wo_ref, lx_ref, wq_ref, km_ref, vm_ref, wxo_ref,
               o_ref):
    gm = gm_ref[...].astype(F32)
    mix = (gm[:, :D_MODEL] * _dot(on_ref[...], wn_ref[...])
           + gm[:, D_MODEL:] * _dot(og_ref[...], wg_ref[...]))
    x1 = x_ref[...] + _dot(mix.astype(BF16), wo_ref[...])
    hx = _rms(x1, lx_ref[...]).astype(BF16)
    q = (_dot(hx, wq_ref[...]) * (XATTN_HEAD_DIM ** -0.5)).astype(BF16)
    outs = []
    for h in range(XATTN_HEADS):
        sl = slice(h * XATTN_HEAD_DIM, (h + 1) * XATTN_HEAD_DIM)
        s = _dot_nt(q[:, sl], km_ref[0, :, sl])
        e = jnp.exp(s - jnp.max(s, axis=-1, keepdims=True))
        l = jnp.sum(e, axis=-1, keepdims=True)
        outs.append((_dot(e.astype(BF16), vm_ref[0, :, sl]) / l).astype(BF16))
    o = jnp.concatenate(outs, axis=-1)
    o_ref[...] = x1 + _dot(o, wxo_ref[...])


def _mixx(x2, o_nsa, o_gla, gm, wn, wg, wo, lx, wq, km, vm, wxo):
    n = x2.shape[0]
    tiles_per_seq = SEQ // TM_WIDE
    tile = lambda w: pl.BlockSpec((TM_WIDE, w), lambda i: (i, 0))
    mblk = pl.BlockSpec((1, MEM_LEN, XATTN_WIDTH), lambda i: (i // tiles_per_seq, 0, 0))
    return pl.pallas_call(
        _mixx_body,
        grid=(n // TM_WIDE,),
        in_specs=[tile(D_MODEL), tile(XATTN_WIDTH), tile(GLA_VAL_WIDTH), tile(2 * D_MODEL),
                  _const_spec((XATTN_WIDTH, D_MODEL)), _const_spec((GLA_VAL_WIDTH, D_MODEL)),
                  _const_spec((D_MODEL, D_MODEL)), _const_spec((1, D_MODEL)),
                  _const_spec((D_MODEL, XATTN_WIDTH)), mblk, mblk, _const_spec((XATTN_WIDTH, D_MODEL))],
        out_specs=tile(D_MODEL),
        out_shape=jax.ShapeDtypeStruct((n, D_MODEL), F32),
        compiler_params=pltpu.CompilerParams(dimension_semantics=("arbitrary",), vmem_limit_bytes=VMEM_LIMIT),
        name="mixx",
    )(x2, o_nsa, o_gla, gm, wn, wg, wo, lx, wq, km, vm, wxo)


def _ffn_body(x_ref, lf_ref, wu_ref, wgt_ref, cw_ref, cb_ref, wd_ref, lfin_ref, o_ref, ubuf_ref, carry_ref):
    @pl.when(pl.program_id(0) % (SEQ // TM) == 0)
    def _():
        carry_ref[...] = jnp.zeros_like(carry_ref)

    x = x_ref[...]
    hf = _rms(x, lf_ref[...]).astype(BF16)
    acc = jnp.zeros((TM, D_MODEL), F32)
    for c0 in range(0, FFN_DIM, FFN_CHUNK):
        cols = slice(c0, min(c0 + FFN_CHUNK, FFN_DIM))
        w = cols.stop - cols.start
        ubuf_ref[0:8, 0:w] = carry_ref[:, cols]
        ubuf_ref[8:, 0:w] = _dot(hf, wu_ref[:, cols])
        carry_ref[:, cols] = ubuf_ref[TM:TM + 8, 0:w]
        u = (cw_ref[0:1, cols] * ubuf_ref[6:6 + TM, 0:w] + cw_ref[1:2, cols] * ubuf_ref[7:7 + TM, 0:w]
             + cw_ref[2:3, cols] * ubuf_ref[8:8 + TM, 0:w] + cb_ref[:, cols])
        act = (jax.nn.gelu(u) * _dot(hf, wgt_ref[:, cols])).astype(BF16)
        acc = acc + _dot(act, wd_ref[cols, :])
    o_ref[...] = _rms(x + acc, lfin_ref[...])


def _ffn(x2, lf, wu, wgt, cw, cb, wd, lfin):
    n = x2.shape[0]
    tile = pl.BlockSpec((TM, D_MODEL), lambda i: (i, 0))
    return pl.pallas_call(
        _ffn_body,
        grid=(n // TM,),
        in_specs=[tile, _const_spec((1, D_MODEL)), _const_spec((D_MODEL, FFN_DIM)), _const_spec((D_MODEL, FFN_DIM)),
                  _const_spec((CONV_WIDTH, FFN_DIM)), _const_spec((1, FFN_DIM)), _const_spec((FFN_DIM, D_MODEL)),
                  _const_spec((1, D_MODEL))],
        out_specs=tile,
        out_shape=jax.ShapeDtypeStruct((n, D_MODEL), F32),
        scratch_shapes=[pltpu.VMEM((TM + 8, FFN_CHUNK), F32), pltpu.VMEM((8, FFN_DIM), F32)],
        compiler_params=pltpu.CompilerParams(dimension_semantics=("arbitrary",), vmem_limit_bytes=VMEM_LIMIT),
        name="ffn",
    )(x2, lf, wu, wgt, cw, cb, wd, lfin)


def _layout_inproj(w_in):
    o = 0
    seg = {}
    for name, wdt in (("q", 512), ("kc", 128), ("vc", 128), ("ks", 128), ("vs", 128), ("kw", 128), ("vw", 128),
                      ("gt", 24), ("qb", 256), ("kb", 256), ("vb", 512), ("rb", 512), ("al", 16), ("mg", 2048)):
        seg[name] = w_in[:, o:o + wdt]
        o += wdt
    q_scale = (HEAD_DIM ** -0.5) * LOG2E
    cols = [seg["q"] * q_scale, seg["ks"], seg["kw"], seg["vs"], seg["vw"], seg["kc"], seg["vc"]]
    cols += [jnp.pad(seg["gt"], ((0, 0), (0, LANE - 3 * NSA_HEADS)))]
    cols += [jnp.pad(seg["al"], ((0, 0), (0, LANE - GLA_RANK)))]
    cols += [seg["qb"] * (GLA_DK ** -0.5), seg["kb"], seg["vb"], seg["rb"]]
    cols += [seg["mg"]]
    return jnp.concatenate(cols, axis=1).astype(BF16)


def _position_constants():
    slopes = 2.0 ** (-np.arange(1, NSA_HEADS + 1, dtype=np.float64)) * LOG2E
    qconst = np.zeros((1, NSA_HEADS * LANE), np.float32)
    for hh in range(NSA_HEADS):
        qconst[0, hh * LANE + AUG_HI] = slopes[hh]
        qconst[0, hh * LANE + AUG_LO] = slopes[hh]
    t = np.arange(SEQ)
    kaug_s = np.zeros((SEQ, 2 * LANE), np.float32)
    kaug_w = np.zeros((SEQ, 2 * LANE), np.float32)
    for h in range(NSA_KV_HEADS):
        kaug_s[t, h * LANE + AUG_SEL + t // SLC_BLOCK] = 1.0
        for a in (kaug_s, kaug_w):
            a[:, h * LANE + AUG_HI] = (t // SLC_BLOCK) * SLC_BLOCK
            a[:, h * LANE + AUG_LO] = t % SLC_BLOCK
    slope_rows = np.zeros((NSA_KV_HEADS, 8, LANE), np.float32)
    rrep = np.zeros((NSA_KV_HEADS, LANE, 6 * LANE), np.float32)
    for h in range(NSA_KV_HEADS):
        for g in range(NSA_GROUP):
            slope_rows[h, g, :] = slopes[h * NSA_GROUP + g]
            for c in range(3):
                pp, half = g // 2, g % 2
                c0 = (c * 2 + pp) * LANE + half * HEAD_DIM
                rrep[h, (h * NSA_GROUP + g) * 3 + c, c0:c0 + HEAD_DIM] = 1.0
    ovl = np.zeros((N_SLC, N_CMP_PAD), np.float32)
    for n in range(N_CMP_PAD - 1):
        for tok in range(n * CMP_STRIDE, n * CMP_STRIDE + CMP_BLOCK):
            ovl[tok // SLC_BLOCK, n] += 1.0 / CMP_BLOCK
    return qconst, kaug_s, kaug_w, slope_rows, rrep, ovl


def _layout_compress(w1, w2, pos, dup):
    w1 = w1.reshape(2, CMP_STRIDE, 1, HEAD_DIM, CMP_HIDDEN)
    sel = jnp.eye(NSA_KV_HEADS, dtype=F32)[:, None, None, :, None, None]
    wh = (w1[None] * sel).reshape(NSA_KV_HEADS, 2, CMP_STRIDE * NSA_KV_HEADS * HEAD_DIM, CMP_HIDDEN)
    p = jnp.broadcast_to(pos.reshape(2, CMP_STRIDE, 1, HEAD_DIM), (2, CMP_STRIDE, NSA_KV_HEADS, HEAD_DIM))
    p = p.reshape(2, 1, CMP_STRIDE * NSA_KV_HEADS * HEAD_DIM)
    w2p = jnp.concatenate([w2, w2 if dup else jnp.zeros_like(w2)], axis=1)
    return wh[:, 0].astype(BF16), wh[:, 1].astype(BF16), w2p.astype(BF16), p[0], p[1]


def kernel(x, mem, ln_mix_g, w_in, nsa_gate_b, cmp_pos_k, cmp_w1_k, cmp_w2_k, cmp_pos_v, cmp_w1_v, cmp_w2_v,
           gla_w_alpha2, gla_b_alpha, gla_norm_g, w_branch_nsa, w_branch_gla, w_out, ln_x_g, ln_mem_g, w_xq,
           w_xkv, w_xo, ln_ffn_g, w_up, conv_w, conv_b, w_down, ln_final_g):
    b = x.shape[0]
    n = b * SEQ
    row = lambda v: v.reshape(1, -1).astype(F32)
    qconst, kaug_s, kaug_w, slope_rows, rrep, ovl = _position_constants()

    x2 = x.reshape(n, D_MODEL)
    gate_b = jnp.pad(nsa_gate_b[0], (0, LANE - 3 * NSA_HEADS)).reshape(1, LANE)
    wa2 = jnp.pad(gla_w_alpha2[0], ((0, LANE - GLA_RANK), (0, 0))).astype(BF16)
    (q, ks, kw, vs, vw, kc, vc, gt, qb, kb, vb, rb, la, gm) = _inproj(
        x2, row(ln_mix_g[0]), _layout_inproj(w_in[0]), qconst, kaug_s, kaug_w, gate_b, wa2, row(gla_b_alpha[0]))

    s3 = lambda a: a.reshape(b, SEQ, a.shape[-1])
    chunked = lambda a: a.reshape(b, N_CMP_PAD, a.shape[-1])
    wak, wbk, w2k, p1k, p2k = _layout_compress(cmp_w1_k[0], cmp_w2_k[0], cmp_pos_k[0], dup=False)
    wav, wbv, w2v, p1v, p2v = _layout_compress(cmp_w1_v[0], cmp_w2_v[0], cmp_pos_v[0], dup=True)
    o_nsa = _nsa(s3(q), s3(ks), s3(kw), s3(vs), s3(vw), chunked(kc), chunked(vc), s3(gt),
                 wak, wbk, w2k, wav, wbv, w2v, p1k, p2k, p1v, p2v,
                 jnp.asarray(rrep, BF16), jnp.asarray(slope_rows), jnp.asarray(ovl, BF16))

    o_gla = _gla(s3(qb), s3(kb), s3(vb), s3(rb), s3(la), row(gla_norm_g[0]))

    km, vm = _memkv(mem, row(ln_mem_g[0]), w_xkv[0].astype(BF16))
    x_mid = _mixx(x2, o_nsa.reshape(n, -1), o_gla.reshape(n, -1), gm,
                  w_branch_nsa[0].astype(BF16), w_branch_gla[0].astype(BF16), w_out[0].astype(BF16),
                  row(ln_x_g[0]), w_xq[0].astype(BF16), km, vm, w_xo[0].astype(BF16))

    out = _ffn(x_mid, row(ln_ffn_g[0]), w_up[0][:, :FFN_DIM].astype(BF16), w_up[0][:, FFN_DIM:].astype(BF16),
               conv_w[0], conv_b[0].reshape(1, FFN_DIM), w_down[0].astype(BF16), row(ln_final_g))
    return out.reshape(b, SEQ, D_MODEL)
```

```python
import math

import jax
import jax.numpy as jnp
import numpy as np
from jax import lax
from jax.experimental import pallas as pl
from jax.experimental.pallas import tpu as pltpu

F32 = jnp.float32
BF16 = jnp.bfloat16

D_MODEL = 1024
SEQ = 2048
MEM_LEN = 256
HEAD_DIM = 64
NSA_HEADS = 8
NSA_KV_HEADS = 2
NSA_GROUP = 4
CMP_BLOCK = 32
CMP_STRIDE = 16
CMP_HIDDEN = 128
N_CMP_PAD = SEQ // CMP_STRIDE
SLC_BLOCK = 64
N_SLC = SEQ // SLC_BLOCK
N_SELECT = 8
WINDOW = 256
GLA_HEADS = 4
GLA_DK = 64
GLA_DV = 128
GLA_KEY_WIDTH = 256
GLA_VAL_WIDTH = 512
GLA_RANK = 16
GLA_TAU = 16.0
GLA_CHUNK = 64
GLA_GROUP = 256
XATTN_HEADS = 4
XATTN_HEAD_DIM = 128
XATTN_WIDTH = 512
FFN_DIM = 2816
CONV_WIDTH = 3
RMS_EPS = 1e-6
NEG_INF = -1e30
LOG2E = math.log2(math.e)

LANE = 128
VMEM_LIMIT = 56 * 1024 * 1024

AUG_SEL = HEAD_DIM
AUG_HI = HEAD_DIM + N_SLC
AUG_LO = AUG_HI + 1

C_Q = 0
C_KS = C_Q + NSA_HEADS * HEAD_DIM
C_KW = C_KS + LANE
C_VS = C_KW + LANE
C_VW = C_VS + LANE
C_KC = C_VW + LANE
C_VC = C_KC + LANE
C_GT = C_VC + LANE
C_AL = C_GT + LANE
C_QB = C_AL + LANE
C_KB = C_QB + GLA_KEY_WIDTH
C_VB = C_KB + GLA_KEY_WIDTH
C_RB = C_VB + GLA_VAL_WIDTH
C_MG = C_RB + GLA_VAL_WIDTH
C_END = C_MG + 2 * D_MODEL

TM = 512
TM_WIDE = 1024
QUAD_TOKENS = 4 * SLC_BLOCK
QUAD_ROWS = NSA_GROUP * QUAD_TOKENS
WIN_PAD = WINDOW
WIN_KEYS = WIN_PAD + QUAD_TOKENS
KEY_CHUNK = 512
MXU_DIM = 256
FFN_CHUNK = 11 * MXU_DIM


def _rms(x, g):
    return x * lax.rsqrt(jnp.mean(x * x, axis=-1, keepdims=True) + RMS_EPS) * g


def _dot(a, b, **kw):
    return jnp.dot(a, b, preferred_element_type=F32, **kw)


def _dot_nt(a, b):
    return lax.dot_general(a, b, (((1,), (1,)), ((), ())), preferred_element_type=F32)


def _dot_tn(a, b):
    return lax.dot_general(a, b, (((0,), (0,)), ((), ())), preferred_element_type=F32)


def _split_bf16(x):
    hi = x.astype(BF16)
    return hi, (x - hi.astype(F32)).astype(BF16)


def _const_spec(shape):
    return pl.BlockSpec(shape, lambda *_: (0,) * len(shape), pipeline_mode=pl.Buffered(1))


def _inproj_body(x_ref, g_ref, w_ref, qc_ref, kas_ref, kaw_ref, gb_ref, wa2_ref, ba_ref,
                 q_o, ks_o, kw_o, vs_o, vw_o, kc_o, vc_o, gt_o, qb_o, kb_o, vb_o, rb_o, la_o, gm_o, chunk_ref):
    h = _rms(x_ref[...], g_ref[...]).astype(BF16)

    def proj(lo, hi):
        return _dot(h, w_ref[:, lo:hi])

    gm_o[...] = jax.nn.sigmoid(proj(C_MG, C_END)).astype(BF16)

    low = lax.broadcasted_iota(jnp.int32, (TM_WIDE, LANE), 1) < HEAD_DIM

    def head_slots(x, fill):
        swapped = pltpu.roll(x, HEAD_DIM, axis=1)
        return [jnp.where(low, x, fill(0, x, swapped)), jnp.where(low, swapped, fill(1, swapped, x))]

    def group(wide, c0, base):
        return wide[:, c0 - base:c0 - base + LANE]

    pq = proj(C_Q, C_KS)
    for pr in range(NSA_HEADS // 2):
        slots = head_slots(group(pq, C_Q + pr * LANE, C_Q),
                           lambda i, mine, other: qc_ref[:, (2 * pr + i) * LANE:(2 * pr + i + 1) * LANE])
        for i, sl in enumerate(slots):
            q_o[:, (2 * pr + i) * LANE:(2 * pr + i + 1) * LANE] = sl.astype(BF16)
    pkv = proj(C_KS, C_KC)
    for c0, aug_ref, out in ((C_KS, kas_ref, ks_o), (C_KW, kaw_ref, kw_o)):
        slots = head_slots(group(pkv, c0, C_KS), lambda i, mine, other: aug_ref[:, i * LANE:(i + 1) * LANE])
        for i, sl in enumerate(slots):
            out[:, i * LANE:(i + 1) * LANE] = sl.astype(BF16)
    for c0, out in ((C_VS, vs_o), (C_VW, vw_o)):
        for i, sl in enumerate(head_slots(group(pkv, c0, C_KS), lambda i, mine, other: other)):
            out[:, i * LANE:(i + 1) * LANE] = sl.astype(BF16)
    pmisc = proj(C_KC, C_QB)
    for c0, out in ((C_KC, kc_o), (C_VC, vc_o)):
        chunk_ref[...] = group(pmisc, c0, C_KC)
        for tk in range(CMP_STRIDE):
            out[:, tk * LANE:(tk + 1) * LANE] = chunk_ref[pl.ds(tk, TM_WIDE // CMP_STRIDE, stride=CMP_STRIDE), :].astype(BF16)
    gt_o[...] = jax.nn.sigmoid(group(pmisc, C_GT, C_KC) + gb_ref[...]).astype(BF16)
    z = _dot(group(pmisc, C_AL, C_KC).astype(BF16), wa2_ref[...]) + ba_ref[...]
    log_sig = jnp.minimum(z, 0.0) - jnp.log1p(jnp.exp(-jnp.abs(z)))
    la_o[...] = log_sig * (1.0 / GLA_TAU)
    qb_o[...] = proj(C_QB, C_KB).astype(BF16)
    kb_o[...] = proj(C_KB, C_VB).astype(BF16)
    vb_o[...] = proj(C_VB, C_RB).astype(BF16)
    rb_o[...] = proj(C_RB, C_MG).astype(BF16)


def _inproj(x2, g, wcat, qconst, kaug_s, kaug_w, gate_b, wa2, ba):
    n = x2.shape[0]
    tiles_per_seq = SEQ // TM_WIDE
    outs = [(1, NSA_HEADS * LANE, BF16), (1, LANE * 2, BF16), (1, LANE * 2, BF16), (1, LANE * 2, BF16),
            (1, LANE * 2, BF16), (CMP_STRIDE, CMP_STRIDE * LANE, BF16), (CMP_STRIDE, CMP_STRIDE * LANE, BF16),
            (1, LANE, BF16), (1, GLA_KEY_WIDTH, BF16), (1, GLA_KEY_WIDTH, BF16),
            (1, GLA_VAL_WIDTH, BF16), (1, GLA_VAL_WIDTH, BF16), (1, GLA_KEY_WIDTH, F32), (1, 2 * D_MODEL, BF16)]
    tile = lambda w, per=1: pl.BlockSpec((TM_WIDE // per, w), lambda i: (i, 0))
    return pl.pallas_call(
        _inproj_body,
        grid=(n // TM_WIDE,),
        in_specs=[tile(D_MODEL), _const_spec((1, D_MODEL)), _const_spec((D_MODEL, C_END)),
                  _const_spec((1, NSA_HEADS * LANE)),
                  pl.BlockSpec((TM_WIDE, 2 * LANE), lambda i: (i % tiles_per_seq, 0)),
                  pl.BlockSpec((TM_WIDE, 2 * LANE), lambda i: (i % tiles_per_seq, 0)),
                  _const_spec((1, LANE)), _const_spec((LANE, GLA_KEY_WIDTH)), _const_spec((1, GLA_KEY_WIDTH))],
        out_specs=[tile(w, per) for per, w, _ in outs],
        out_shape=[jax.ShapeDtypeStruct((n // per, w), dt) for per, w, dt in outs],
        scratch_shapes=[pltpu.VMEM((TM_WIDE, LANE), F32)],
        compiler_params=pltpu.CompilerParams(dimension_semantics=("arbitrary",), vmem_limit_bytes=VMEM_LIMIT),
        name="inproj",
    )(x2, g, wcat, qconst, kaug_s, kaug_w, gate_b, wa2, ba)


def _memkv_body(m_ref, g_ref, w_ref, k_o, v_o):
    hm = _rms(m_ref[0], g_ref[...]).astype(BF16)
    kv = _dot(hm, w_ref[...])
    k_o[0] = kv[:, :XATTN_WIDTH].astype(BF16)
    v_o[0] = kv[:, XATTN_WIDTH:].astype(BF16)


def _memkv(mem, g, w):
    b = mem.shape[0]
    blk = pl.BlockSpec((1, MEM_LEN, XATTN_WIDTH), lambda i: (i, 0, 0))
    return pl.pallas_call(
        _memkv_body,
        grid=(b,),
        in_specs=[pl.BlockSpec((1, MEM_LEN, D_MODEL), lambda i: (i, 0, 0)), _const_spec((1, D_MODEL)),
                  _const_spec((D_MODEL, 2 * XATTN_WIDTH))],
        out_specs=[blk, blk],
        out_shape=[jax.ShapeDtypeStruct((b, MEM_LEN, XATTN_WIDTH), BF16)] * 2,
        compiler_params=pltpu.CompilerParams(dimension_semantics=("arbitrary",)),
        name="memkv",
    )(mem, g, w)


def _compress(c_ref, wa_ref, wb_ref, w2_ref, p1_ref, p2_ref):
    c = c_ref[0].astype(F32)
    a = _dot((c + p1_ref[...]).astype(BF16), wa_ref[0])
    b = _dot((c + p2_ref[...]).astype(BF16), wb_ref[0])
    hid = a + pltpu.roll(b, N_CMP_PAD - 1, axis=0)
    return _dot(jax.nn.gelu(hid).astype(BF16), w2_ref[...]).astype(BF16)


def _nsa_constants(slope_ref, biasc_ref, wbias_ref, dbias_ref):
    t_col = lax.broadcasted_iota(jnp.int32, (SEQ, LANE), 0)
    n_row = lax.broadcasted_iota(jnp.int32, (SEQ, LANE), 1)
    dist_c = (t_col - (n_row * CMP_STRIDE + (CMP_BLOCK - 1))).astype(F32)
    for g in range(NSA_GROUP):
        biasc_ref[g] = jnp.where(dist_c >= 0.0, -slope_ref[0, g:g + 1, :] * dist_c, NEG_INF)
    col = lax.broadcasted_iota(jnp.int32, (QUAD_ROWS, WIN_KEYS), 1)
    row = lax.broadcasted_iota(jnp.int32, (QUAD_ROWS, WIN_KEYS), 0)
    dist_w = ((row >> 8) << 6) + (row & (SLC_BLOCK - 1)) + WIN_PAD - col
    wbias_ref[...] = jnp.where((dist_w >= 0) & (dist_w < WINDOW), 0.0, NEG_INF)
    col = lax.broadcasted_iota(jnp.int32, (QUAD_ROWS, QUAD_TOKENS), 1)
    row = lax.broadcasted_iota(jnp.int32, (QUAD_ROWS, QUAD_TOKENS), 0)
    blk_r, blk_c = row >> 8, col >> 6
    causal = (col & (SLC_BLOCK - 1)) <= (row & (SLC_BLOCK - 1))
    dbias_ref[...] = jnp.where((blk_c < blk_r) | ((blk_c == blk_r) & causal), 0.0, NEG_INF)


def _nsa_body(q_ref, ks_ref, kw_ref, vs_ref, vw_ref, kc_ref, vc_ref, gt_ref,
              wak_ref, wbk_ref, w2k_ref, wav_ref, wbv_ref, w2v_ref, p1k_ref, p2k_ref, p1v_ref, p2v_ref,
              rrep_ref, slope_ref, ovl_ref, o_ref,
              ocmp_ref, qs_ref, biasc_ref, wbias_ref, dbias_ref, s_ref, e_ref, sw_ref, ew_ref):
    @pl.when(pl.program_id(1) == 0)
    def _():
        _nsa_constants(slope_ref, biasc_ref, wbias_ref, dbias_ref)

    kcmp = _compress(kc_ref, wak_ref, wbk_ref, w2k_ref, p1k_ref, p2k_ref)
    vcmp = _compress(vc_ref, wav_ref, wbv_ref, w2v_ref, p1v_ref, p2v_ref)

    low_half = lax.broadcasted_iota(jnp.int32, (SEQ, LANE), 1) < HEAD_DIM
    has_key = lax.broadcasted_iota(jnp.int32, (SEQ, 1), 0) >= CMP_BLOCK - 1
    imp = jnp.zeros((SEQ, LANE), F32)
    ones_sq = jnp.ones((LANE, LANE), BF16)
    o_even = None
    for g in range(NSA_GROUP):
        s = _dot_nt(q_ref[0, :, g * LANE:(g + 1) * LANE], kcmp) + biasc_ref[g]
        e = jnp.exp2(s - jnp.max(s, axis=-1, keepdims=True))
        l = _dot(e.astype(BF16), ones_sq)
        p = e * jnp.where(has_key, 1.0 / l, 0.0)
        imp = imp + p
        oc = _dot(p.astype(BF16), vcmp)
        if g % 2 == 0:
            o_even = oc
        else:
            ocmp_ref[g // 2] = jnp.where(low_half, o_even, oc)

    hi, lo = _split_bf16(imp.T)
    ovl = ovl_ref[...]
    imp_b = _dot(ovl, hi) + _dot(ovl, lo)
    j = lax.broadcasted_iota(jnp.int32, (N_SLC, SEQ), 0)
    cur = lax.broadcasted_iota(jnp.int32, (N_SLC, SEQ), 1) >> 6
    forced = (j == 0) | (j == cur) | (j == cur - 1)
    val = jnp.where(forced, float(NSA_GROUP + 1), jnp.where(j > cur, -1.0, imp_b))
    j8 = lax.broadcasted_iota(jnp.int32, (8, LANE), 0)
    rank_tiles = []
    for lt in range(SEQ // LANE):
        n_blk = (lt + 1) * LANE // SLC_BLOCK
        n_grp = -(-n_blk // 8)
        vt = val[:, lt * LANE:(lt + 1) * LANE]
        vals = [vt[8 * r:8 * r + 8, :] for r in range(n_grp)]
        ranks = [jnp.zeros((8, LANE), F32) for _ in vals]
        for jp in range(n_blk):
            vj = vt[jp:jp + 1, :]
            for r, vr in enumerate(vals):
                if 8 * r + 7 <= jp:
                    beats = vj > vr
                elif 8 * r > jp:
                    beats = vj >= vr
                else:
                    beats = (vj > vr) | ((vj == vr) & (j8 > jp - 8 * r))
                ranks[r] = ranks[r] + jnp.where(beats, 1.0, 0.0)
        ranks += [jnp.zeros((8, LANE), F32)] * (N_SLC // 8 - n_grp)
        rank_tiles.append(jnp.concatenate(ranks, axis=0))
    rank = jnp.concatenate(rank_tiles, axis=1)
    sel_bias = jnp.where((rank < float(N_SELECT)) & (j <= cur), 0.0, NEG_INF)
    aug_t = jnp.concatenate([jnp.zeros((AUG_SEL, SEQ), F32), sel_bias,
                             jnp.zeros((LANE - AUG_SEL - N_SLC, SEQ), F32)], axis=0)
    aug = aug_t.T.astype(BF16)
    for g in range(NSA_GROUP):
        qs_ref[g] = q_ref[0, :, g * LANE:(g + 1) * LANE] + aug

    lane64 = lax.broadcasted_iota(jnp.int32, (SLC_BLOCK, LANE), 1) < HEAD_DIM

    def softmax_pv(s_buf, e_buf, rows, nkeys, v):
        m = jnp.max(s_buf[rows, 0:nkeys], axis=-1, keepdims=True)
        e = jnp.exp2(s_buf[rows, 0:nkeys] - m)
        e_buf[rows, 0:nkeys] = e.astype(BF16)
        return _dot(e_buf[rows, 0:nkeys], v) * (1.0 / jnp.sum(e, axis=-1, keepdims=True))

    def quad(iq):
        r0 = iq * QUAD_TOKENS
        nkeys = r0 + QUAD_TOKENS
        qs = jnp.concatenate([qs_ref[g, r0 + bq * SLC_BLOCK:r0 + (bq + 1) * SLC_BLOCK, :]
                              for bq in range(QUAD_TOKENS // SLC_BLOCK) for g in range(NSA_GROUP)], axis=0)
        halves = [slice(hf * QUAD_ROWS // 2, (hf + 1) * QUAD_ROWS // 2) for hf in range(2)]
        for rows in halves:
            for c0 in range(0, r0, KEY_CHUNK):
                c1 = min(c0 + KEY_CHUNK, r0)
                s_ref[rows, c0:c1] = _dot_nt(qs[rows], ks_ref[0, c0:c1, :])
            s_ref[rows, r0:nkeys] = _dot_nt(qs[rows], ks_ref[0, r0:nkeys, :]) + dbias_ref[rows, :]
        w0 = max(r0 - WIN_PAD, 0)
        nwin = r0 + QUAD_TOKENS - w0
        sw_ref[:, 0:nwin] = _dot_nt(qs, kw_ref[0, w0:w0 + nwin, :]) + wbias_ref[:, WIN_KEYS - nwin:]
        o_s = jnp.concatenate([softmax_pv(s_ref, e_ref, rows, nkeys, vs_ref[0, 0:nkeys, :]) for rows in halves],
                              axis=0)
        o_w = softmax_pv(sw_ref, ew_ref, slice(None), nwin, vw_ref[0, w0:w0 + nwin, :])
        gt = _dot(gt_ref[0, r0:r0 + QUAD_TOKENS, :], rrep_ref[0])
        for bq in range(QUAD_TOKENS // SLC_BLOCK):
            tok = slice(r0 + bq * SLC_BLOCK, r0 + (bq + 1) * SLC_BLOCK)
            gtb = gt[bq * SLC_BLOCK:(bq + 1) * SLC_BLOCK]
            for pp in range(2):
                a0 = (bq * NSA_GROUP + 2 * pp) * SLC_BLOCK
                a1, a2 = a0 + SLC_BLOCK, a0 + 2 * SLC_BLOCK
                osl = jnp.where(lane64, o_s[a0:a1], o_s[a1:a2])
                owi = jnp.where(lane64, o_w[a0:a1], o_w[a1:a2])
                out = (gtb[:, pp * LANE:(pp + 1) * LANE] * ocmp_ref[pp, tok, :]
                       + gtb[:, (2 + pp) * LANE:(3 + pp) * LANE] * osl
                       + gtb[:, (4 + pp) * LANE:(5 + pp) * LANE] * owi)
                o_ref[0, tok, pp * LANE:(pp + 1) * LANE] = out.astype(BF16)

    one = jnp.minimum(pl.program_id(1) + 1, 1)
    for iq in range(SEQ // QUAD_TOKENS):
        def body(_, carry, iq=iq):
            quad(iq)
            return carry

        lax.fori_loop(0, one, body, 0)


def _nsa(q, ks, kw, vs, vw, kc2, vc2, gt, wak, wbk, w2k, wav, wbv, w2v, p1k, p2k, p1v, p2v, rrep, slopes, ovl):
    b = q.shape[0]
    hs = lambda w: pl.BlockSpec((1, SEQ, w), lambda h, i: (i, 0, h))
    perh = lambda s: pl.BlockSpec((1,) + s, lambda h, i: (h,) + (0,) * len(s))
    cblk = pl.BlockSpec((1, N_CMP_PAD, CMP_STRIDE * 2 * HEAD_DIM), lambda h, i: (i, 0, 0))
    w1 = (CMP_STRIDE * 2 * HEAD_DIM, CMP_HIDDEN)
    return pl.pallas_call(
        _nsa_body,
        grid=(NSA_KV_HEADS, b),
        in_specs=[hs(NSA_GROUP * LANE), hs(LANE), hs(LANE), hs(LANE), hs(LANE), cblk, cblk,
                  pl.BlockSpec((1, SEQ, LANE), lambda h, i: (i, 0, 0)),
                  perh(w1), perh(w1), _const_spec((CMP_HIDDEN, LANE)),
                  perh(w1), perh(w1), _const_spec((CMP_HIDDEN, LANE)),
                  _const_spec((1, w1[0])), _const_spec((1, w1[0])), _const_spec((1, w1[0])), _const_spec((1, w1[0])),
                  perh((LANE, 6 * LANE)), perh((8, LANE)), _const_spec((N_SLC, LANE))],
        out_specs=pl.BlockSpec((1, SEQ, 2 * LANE), lambda h, i: (i, 0, h)),
        out_shape=jax.ShapeDtypeStruct((b, SEQ, NSA_HEADS * HEAD_DIM), BF16),
        scratch_shapes=[pltpu.VMEM((2, SEQ, LANE), F32), pltpu.VMEM((NSA_GROUP, SEQ, LANE), BF16),
                        pltpu.VMEM((NSA_GROUP, SEQ, LANE), F32), pltpu.VMEM((QUAD_ROWS, WIN_KEYS), F32),
                        pltpu.VMEM((QUAD_ROWS, QUAD_TOKENS), F32),
                        pltpu.VMEM((QUAD_ROWS, SEQ), F32), pltpu.VMEM((QUAD_ROWS, SEQ), BF16),
                        pltpu.VMEM((QUAD_ROWS, WIN_KEYS), F32), pltpu.VMEM((QUAD_ROWS, WIN_KEYS), BF16)],
        compiler_params=pltpu.CompilerParams(dimension_semantics=("arbitrary", "arbitrary"),
                                             vmem_limit_bytes=VMEM_LIMIT),
        name="nsa",
    )(q, ks, kw, vs, vw, kc2, vc2, gt, wak, wbk, w2k, wav, wbv, w2v, p1k, p2k, p1v, p2v, rrep, slopes, ovl)


def _gla_body(q_ref, k_ref, v_ref, r_ref, la_ref, ng_ref, o_ref, oi_ref, qd_ref, inc_ref, dec_ref, st_ref):
    n_chunks = SEQ // GLA_CHUNK
    per_group = GLA_GROUP // GLA_CHUNK
    ri = lax.broadcasted_iota(jnp.int32, (GLA_GROUP, GLA_GROUP), 0)
    ci = lax.broadcasted_iota(jnp.int32, (GLA_GROUP, GLA_GROUP), 1)
    same_chunk = (ri >> 6) == (ci >> 6)
    tri = same_chunk & (ci <= ri)
    t_mat = jnp.where(tri, 1.0, 0.0).astype(BF16)
    o_mat = jnp.where(same_chunk, 1.0, 0.0).astype(BF16)
    head_g = lax.broadcasted_iota(jnp.int32, (GLA_GROUP, GLA_KEY_WIDTH), 1) >> 6
    head_c = lax.broadcasted_iota(jnp.int32, (GLA_CHUNK, GLA_KEY_WIDTH), 1) >> 6
    head_s = lax.broadcasted_iota(jnp.int32, (GLA_DV, GLA_KEY_WIDTH), 1) >> 6

    def group(gi):
        rows = slice(gi * GLA_GROUP, (gi + 1) * GLA_GROUP)
        hi, lo = _split_bf16(la_ref[0, rows, :])
        b = _dot(t_mat, hi) + _dot(t_mat, lo)
        b_last = _dot(o_mat, hi) + _dot(o_mat, lo)
        k = k_ref[0, rows, :].astype(F32)
        q_d = q_ref[0, rows, :].astype(F32) * jnp.exp(b)
        k_d = (k * jnp.exp(-b)).astype(BF16)
        k_s = (k * jnp.exp(b_last - b)).astype(BF16)
        qd_ref[rows, :] = q_d.astype(BF16)
        v = v_ref[0, rows, :]
        for h in range(GLA_HEADS):
            qm = jnp.where(head_g == h, q_d, 0.0).astype(BF16)
            att = jnp.where(tri, _dot_nt(qm, k_d), 0.0)
            oi_ref[rows, h * GLA_DV:(h + 1) * GLA_DV] = _dot(att.astype(BF16), v[:, h * GLA_DV:(h + 1) * GLA_DV])
        decay = jnp.exp(b_last)
        for c in range(per_group):
            cr = slice(c * GLA_CHUNK, (c + 1) * GLA_CHUNK)
            inc_full = _dot_tn(v[cr, :], k_s[cr, :])
            inc = jnp.zeros((GLA_DV, GLA_KEY_WIDTH), F32)
            for h in range(GLA_HEADS):
                inc = inc + jnp.where(head_s == h, inc_full[h * GLA_DV:(h + 1) * GLA_DV, :], 0.0)
            inc_ref[gi * per_group + c] = inc
            dec_ref[gi * per_group + c] = decay[c * GLA_CHUNK:c * GLA_CHUNK + 8, :]

    def emit(gi):
        for c in range(per_group):
            rows = slice(gi * GLA_GROUP + c * GLA_CHUNK, gi * GLA_GROUP + (c + 1) * GLA_CHUNK)
            q_d = qd_ref[rows, :]
            st = st_ref[gi * per_group + c]
            for h in range(GLA_HEADS):
                cols = slice(h * GLA_DV, (h + 1) * GLA_DV)
                qm = jnp.where(head_c == h, q_d, jnp.zeros_like(q_d))
                o = oi_ref[rows, cols] + _dot_nt(qm, st)
                o = o * lax.rsqrt(jnp.mean(o * o, axis=-1, keepdims=True) + RMS_EPS) * ng_ref[...]
                r = r_ref[0, rows, cols].astype(F32)
                o_ref[0, rows, cols] = (o * (r * jax.nn.sigmoid(r))).astype(BF16)

    state = jnp.zeros((GLA_DV, GLA_KEY_WIDTH), F32)
    for gi in range(SEQ // GLA_GROUP):
        group(gi)
        for n in range(gi * per_group, (gi + 1) * per_group):
            st_ref[n] = state.astype(BF16)
            state = state * dec_ref[n, 0:1, :] + inc_ref[n]
        emit(gi)


def _gla(qb, kb, vb, rb, la, ng):
    b = qb.shape[0]
    n_chunks = SEQ // GLA_CHUNK
    blk = lambda w: pl.BlockSpec((1, SEQ, w), lambda i: (i, 0, 0))
    return pl.pallas_call(
        _gla_body,
        grid=(b,),
        in_specs=[blk(GLA_KEY_WIDTH), blk(GLA_KEY_WIDTH), blk(GLA_VAL_WIDTH), blk(GLA_VAL_WIDTH),
                  blk(GLA_KEY_WIDTH), _const_spec((1, GLA_DV))],
        out_specs=blk(GLA_VAL_WIDTH),
        out_shape=jax.ShapeDtypeStruct((b, SEQ, GLA_VAL_WIDTH), BF16),
        scratch_shapes=[pltpu.VMEM((SEQ, GLA_VAL_WIDTH), F32), pltpu.VMEM((SEQ, GLA_KEY_WIDTH), BF16),
                        pltpu.VMEM((n_chunks, GLA_DV, GLA_KEY_WIDTH), F32),
                        pltpu.VMEM((n_chunks, 8, GLA_KEY_WIDTH), F32),
                        pltpu.VMEM((n_chunks, GLA_DV, GLA_KEY_WIDTH), BF16)],
        compiler_params=pltpu.CompilerParams(dimension_semantics=("arbitrary",), vmem_limit_bytes=VMEM_LIMIT),
        name="gla",
    )(qb, kb, vb, rb, la, ng)


def _mixx_math(x_ref, on_ref, og_ref, gm_ref, wn_ref, wg_ref, wo_ref, lx_ref, wq_ref, km_ref, vm_ref, wxo_ref):
    gm = gm_ref[...].astype(F32)
    mix = (gm[:, :D_MODEL] * _dot(on_ref[...], wn_ref[...])
           + gm[:, D_MODEL:] * _dot(og_ref[...], wg_ref[...]))
    x1 = x_ref[...] + _dot(mix.astype(BF16), wo_ref[...])
    hx = _rms(x1, lx_ref[...]).astype(BF16)
    q = (_dot(hx, wq_ref[...]) * (XATTN_HEAD_DIM ** -0.5)).astype(BF16)
    outs = []
    for h in range(XATTN_HEADS):
        sl = slice(h * XATTN_HEAD_DIM, (h + 1) * XATTN_HEAD_DIM)
        s = _dot_nt(q[:, sl], km_ref[0, :, sl])
        e = jnp.exp(s - jnp.max(s, axis=-1, keepdims=True))
        l = jnp.sum(e, axis=-1, keepdims=True)
        outs.append((_dot(e.astype(BF16), vm_ref[0, :, sl]) / l).astype(BF16))
    o = jnp.concatenate(outs, axis=-1)
    return x1 + _dot(o, wxo_ref[...])


def _mixffn_body(x_ref, on_ref, og_ref, gm_ref, wn_ref, wg_ref, wo_ref, lx_ref, wq_ref, km_ref, vm_ref, wxo_ref,
                 lf_ref, wu_ref, wgt_ref, cw_ref, cb_ref, wd_ref, lfin_ref, o_ref, ubuf_ref, carry_ref):
    @pl.when(pl.program_id(0) % (SEQ // TM) == 0)
    def _():
        carry_ref[...] = jnp.zeros_like(carry_ref)

    x = _mixx_math(x_ref, on_ref, og_ref, gm_ref, wn_ref, wg_ref, wo_ref, lx_ref, wq_ref, km_ref, vm_ref, wxo_ref)
    hf = _rms(x, lf_ref[...]).astype(BF16)
    acc = jnp.zeros((TM, D_MODEL), F32)
    for c0 in range(0, FFN_DIM, FFN_CHUNK):
        cols = slice(c0, min(c0 + FFN_CHUNK, FFN_DIM))
        w = cols.stop - cols.start
        ubuf_ref[0:8, 0:w] = carry_ref[:, cols]
        ubuf_ref[8:, 0:w] = _dot(hf, wu_ref[:, cols])
        carry_ref[:, cols] = ubuf_ref[TM:TM + 8, 0:w]
        u = (cw_ref[0:1, cols] * ubuf_ref[6:6 + TM, 0:w] + cw_ref[1:2, cols] * ubuf_ref[7:7 + TM, 0:w]
             + cw_ref[2:3, cols] * ubuf_ref[8:8 + TM, 0:w] + cb_ref[:, cols])
        act = (jax.nn.gelu(u) * _dot(hf, wgt_ref[:, cols])).astype(BF16)
        acc = acc + _dot(act, wd_ref[cols, :])
    o_ref[...] = _rms(x + acc, lfin_ref[...])


def _mixffn(x2, o_nsa, o_gla, gm, wn, wg, wo, lx, wq, km, vm, wxo, lf, wu, wgt, cw, cb, wd, lfin):
    n = x2.shape[0]
    tiles_per_seq = SEQ // TM
    tile = lambda w: pl.BlockSpec((TM, w), lambda i: (i, 0))
    mblk = pl.BlockSpec((1, MEM_LEN, XATTN_WIDTH), lambda i: (i // tiles_per_seq, 0, 0))
    return pl.pallas_call(
        _mixffn_body,
        grid=(n // TM,),
        in_specs=[tile(D_MODEL), tile(XATTN_WIDTH), tile(GLA_VAL_WIDTH), tile(2 * D_MODEL),
                  _const_spec((XATTN_WIDTH, D_MODEL)), _const_spec((GLA_VAL_WIDTH, D_MODEL)),
                  _const_spec((D_MODEL, D_MODEL)), _const_spec((1, D_MODEL)),
                  _const_spec((D_MODEL, XATTN_WIDTH)), mblk, mblk, _const_spec((XATTN_WIDTH, D_MODEL)),
                  _const_spec((1, D_MODEL)), _const_spec((D_MODEL, FFN_DIM)), _const_spec((D_MODEL, FFN_DIM)),
                  _const_spec((CONV_WIDTH, FFN_DIM)), _const_spec((1, FFN_DIM)), _const_spec((FFN_DIM, D_MODEL)),
                  _const_spec((1, D_MODEL))],
        out_specs=tile(D_MODEL),
        out_shape=jax.ShapeDtypeStruct((n, D_MODEL), F32),
        scratch_shapes=[pltpu.VMEM((TM + 8, FFN_CHUNK), F32), pltpu.VMEM((8, FFN_DIM), F32)],
        compiler_params=pltpu.CompilerParams(dimension_semantics=("arbitrary",), vmem_limit_bytes=VMEM_LIMIT),
        name="mixffn",
    )(x2, o_nsa, o_gla, gm, wn, wg, wo, lx, wq, km, vm, wxo, lf, wu, wgt, cw, cb, wd, lfin)


def _layout_inproj(w_in):
    o = 0
    seg = {}
    for name, wdt in (("q", 512), ("kc", 128), ("vc", 128), ("ks", 128), ("vs", 128), ("kw", 128), ("vw", 128),
                      ("gt", 24), ("qb", 256), ("kb", 256), ("vb", 512), ("rb", 512), ("al", 16), ("mg", 2048)):
        seg[name] = w_in[:, o:o + wdt]
        o += wdt
    q_scale = (HEAD_DIM ** -0.5) * LOG2E
    cols = [seg["q"] * q_scale, seg["ks"], seg["kw"], seg["vs"], seg["vw"], seg["kc"], seg["vc"]]
    cols += [jnp.pad(seg["gt"], ((0, 0), (0, LANE - 3 * NSA_HEADS)))]
    cols += [jnp.pad(seg["al"], ((0, 0), (0, LANE - GLA_RANK)))]
    cols += [seg["qb"] * (GLA_DK ** -0.5), seg["kb"], seg["vb"], seg["rb"]]
    cols += [seg["mg"]]
    return jnp.concatenate(cols, axis=1).astype(BF16)


def _position_constants():
    slopes = 2.0 ** (-np.arange(1, NSA_HEADS + 1, dtype=np.float64)) * LOG2E
    qconst = np.zeros((1, NSA_HEADS * LANE), np.float32)
    for hh in range(NSA_HEADS):
        qconst[0, hh * LANE + AUG_HI] = slopes[hh]
        qconst[0, hh * LANE + AUG_LO] = slopes[hh]
    t = np.arange(SEQ)
    kaug_s = np.zeros((SEQ, 2 * LANE), np.float32)
    kaug_w = np.zeros((SEQ, 2 * LANE), np.float32)
    for h in range(NSA_KV_HEADS):
        kaug_s[t, h * LANE + AUG_SEL + t // SLC_BLOCK] = 1.0
        for a in (kaug_s, kaug_w):
            a[:, h * LANE + AUG_HI] = (t // SLC_BLOCK) * SLC_BLOCK
            a[:, h * LANE + AUG_LO] = t % SLC_BLOCK
    slope_rows = np.zeros((NSA_KV_HEADS, 8, LANE), np.float32)
    rrep = np.zeros((NSA_KV_HEADS, LANE, 6 * LANE), np.float32)
    for h in range(NSA_KV_HEADS):
        for g in range(NSA_GROUP):
            slope_rows[h, g, :] = slopes[h * NSA_GROUP + g]
            for c in range(3):
                pp, half = g // 2, g % 2
                c0 = (c * 2 + pp) * LANE + half * HEAD_DIM
                rrep[h, (h * NSA_GROUP + g) * 3 + c, c0:c0 + HEAD_DIM] = 1.0
    ovl = np.zeros((N_SLC, N_CMP_PAD), np.float32)
    for n in range(N_CMP_PAD - 1):
        for tok in range(n * CMP_STRIDE, n * CMP_STRIDE + CMP_BLOCK):
            ovl[tok // SLC_BLOCK, n] += 1.0 / CMP_BLOCK
    return qconst, kaug_s, kaug_w, slope_rows, rrep, ovl


def _layout_compress(w1, w2, pos, dup):
    w1 = w1.reshape(2, CMP_STRIDE, 1, HEAD_DIM, CMP_HIDDEN)
    sel = jnp.eye(NSA_KV_HEADS, dtype=F32)[:, None, None, :, None, None]
    wh = (w1[None] * sel).reshape(NSA_KV_HEADS, 2, CMP_STRIDE * NSA_KV_HEADS * HEAD_DIM, CMP_HIDDEN)
    p = jnp.broadcast_to(pos.reshape(2, CMP_STRIDE, 1, HEAD_DIM), (2, CMP_STRIDE, NSA_KV_HEADS, HEAD_DIM))
    p = p.reshape(2, 1, CMP_STRIDE * NSA_KV_HEADS * HEAD_DIM)
    w2p = jnp.concatenate([w2, w2 if dup else jnp.zeros_like(w2)], axis=1)
    return wh[:, 0].astype(BF16), wh[:, 1].astype(BF16), w2p.astype(BF16), p[0], p[1]


def kernel(x, mem, ln_mix_g, w_in, nsa_gate_b, cmp_pos_k, cmp_w1_k, cmp_w2_k, cmp_pos_v, cmp_w1_v, cmp_w2_v,
           gla_w_alpha2, gla_b_alpha, gla_norm_g, w_branch_nsa, w_branch_gla, w_out, ln_x_g, ln_mem_g, w_xq,
           w_xkv, w_xo, ln_ffn_g, w_up, conv_w, conv_b, w_down, ln_final_g):
    b = x.shape[0]
    n = b * SEQ
    row = lambda v: v.reshape(1, -1).astype(F32)
    qconst, kaug_s, kaug_w, slope_rows, rrep, ovl = _position_constants()

    x2 = x.reshape(n, D_MODEL)
    gate_b = jnp.pad(nsa_gate_b[0], (0, LANE - 3 * NSA_HEADS)).reshape(1, LANE)
    wa2 = jnp.pad(gla_w_alpha2[0], ((0, LANE - GLA_RANK), (0, 0))).astype(BF16)
    (q, ks, kw, vs, vw, kc, vc, gt, qb, kb, vb, rb, la, gm) = _inproj(
        x2, row(ln_mix_g[0]), _layout_inproj(w_in[0]), qconst, kaug_s, kaug_w, gate_b, wa2, row(gla_b_alpha[0]))

    s3 = lambda a: a.reshape(b, SEQ, a.shape[-1])
    chunked = lambda a: a.reshape(b, N_CMP_PAD, a.shape[-1])
    wak, wbk, w2k, p1k, p2k = _layout_compress(cmp_w1_k[0], cmp_w2_k[0], cmp_pos_k[0], dup=False)
    wav, wbv, w2v, p1v, p2v = _layout_compress(cmp_w1_v[0], cmp_w2_v[0], cmp_pos_v[0], dup=True)
    o_nsa = _nsa(s3(q), s3(ks), s3(kw), s3(vs), s3(vw), chunked(kc), chunked(vc), s3(gt),
                 wak, wbk, w2k, wav, wbv, w2v, p1k, p2k, p1v, p2v,
                 jnp.asarray(rrep, BF16), jnp.asarray(slope_rows), jnp.asarray(ovl, BF16))

    o_gla = _gla(s3(qb), s3(kb), s3(vb), s3(rb), s3(la), row(gla_norm_g[0]))

    km, vm = _memkv(mem, row(ln_mem_g[0]), w_xkv[0].astype(BF16))
    out = _mixffn(x2, o_nsa.reshape(n, -1), o_gla.reshape(n, -1), gm,
                  w_branch_nsa[0].astype(BF16), w_branch_gla[0].astype(BF16), w_out[0].astype(BF16),
                  row(ln_x_g[0]), w_xq[0].astype(BF16), km, vm, w_xo[0].astype(BF16),
                  row(ln_ffn_g[0]), w_up[0][:, :FFN_DIM].astype(BF16), w_up[0][:, FFN_DIM:].astype(BF16),
                  conv_w[0], conv_b[0].reshape(1, FFN_DIM), w_down[0].astype(BF16), row(ln_final_g))
    return out.reshape(b, SEQ, D_MODEL)
```

```python
import math

import jax
import jax.numpy as jnp
import numpy as np
from jax import lax
from jax.experimental import pallas as pl
from jax.experimental.pallas import tpu as pltpu

F32 = jnp.float32
BF16 = jnp.bfloat16

D_MODEL = 1024
SEQ = 2048
MEM_LEN = 256
HEAD_DIM = 64
NSA_HEADS = 8
NSA_KV_HEADS = 2
NSA_GROUP = 4
CMP_BLOCK = 32
CMP_STRIDE = 16
CMP_HIDDEN = 128
N_CMP_PAD = SEQ // CMP_STRIDE
SLC_BLOCK = 64
N_SLC = SEQ // SLC_BLOCK
N_SELECT = 8
WINDOW = 256
GLA_HEADS = 4
GLA_DK = 64
GLA_DV = 128
GLA_KEY_WIDTH = 256
GLA_VAL_WIDTH = 512
GLA_RANK = 16
GLA_TAU = 16.0
GLA_CHUNK = 64
GLA_GROUP = 256
XATTN_HEADS = 4
XATTN_HEAD_DIM = 128
XATTN_WIDTH = 512
FFN_DIM = 2816
CONV_WIDTH = 3
RMS_EPS = 1e-6
NEG_INF = -1e30
LOG2E = math.log2(math.e)

LANE = 128
VMEM_LIMIT = 56 * 1024 * 1024

AUG_SEL = HEAD_DIM
AUG_HI = HEAD_DIM + N_SLC
AUG_LO = AUG_HI + 1

C_Q = 0
C_KS = C_Q + NSA_HEADS * HEAD_DIM
C_KW = C_KS + LANE
C_VS = C_KW + LANE
C_VW = C_VS + LANE
C_KC = C_VW + LANE
C_VC = C_KC + LANE
C_GT = C_VC + LANE
C_AL = C_GT + LANE
C_QB = C_AL + LANE
C_KB = C_QB + GLA_KEY_WIDTH
C_VB = C_KB + GLA_KEY_WIDTH
C_RB = C_VB + GLA_VAL_WIDTH
C_MG = C_RB + GLA_VAL_WIDTH
C_END = C_MG + 2 * D_MODEL

TM = 512
TM_WIDE = 1024
QUAD_TOKENS = 4 * SLC_BLOCK
QUAD_ROWS = NSA_GROUP * QUAD_TOKENS
WIN_PAD = WINDOW
WIN_KEYS = WIN_PAD + QUAD_TOKENS
KEY_CHUNK = 512
MXU_DIM = 256
FFN_CHUNK = 11 * MXU_DIM


def _rms(x, g):
    return x * lax.rsqrt(jnp.mean(x * x, axis=-1, keepdims=True) + RMS_EPS) * g


def _dot(a, b, **kw):
    return jnp.dot(a, b, preferred_element_type=F32, **kw)


def _dot_nt(a, b):
    return lax.dot_general(a, b, (((1,), (1,)), ((), ())), preferred_element_type=F32)


def _dot_tn(a, b):
    return lax.dot_general(a, b, (((0,), (0,)), ((), ())), preferred_element_type=F32)


def _split_bf16(x):
    hi = x.astype(BF16)
    return hi, (x - hi.astype(F32)).astype(BF16)


def _const_spec(shape):
    return pl.BlockSpec(shape, lambda *_: (0,) * len(shape), pipeline_mode=pl.Buffered(1))


def _inproj_body(x_ref, g_ref, w_ref, qc_ref, kas_ref, kaw_ref, gb_ref, wa2_ref, ba_ref,
                 q_o, ks_o, kw_o, vs_o, vw_o, kc_o, vc_o, gt_o, qb_o, kb_o, vb_o, rb_o, la_o, gm_o, chunk_ref):
    h = _rms(x_ref[...], g_ref[...]).astype(BF16)

    def proj(lo, hi):
        return _dot(h, w_ref[:, lo:hi])

    gm_o[...] = jax.nn.sigmoid(proj(C_MG, C_END)).astype(BF16)

    low = lax.broadcasted_iota(jnp.int32, (TM_WIDE, LANE), 1) < HEAD_DIM

    def head_slots(x, fill):
        swapped = pltpu.roll(x, HEAD_DIM, axis=1)
        return [jnp.where(low, x, fill(0, x, swapped)), jnp.where(low, swapped, fill(1, swapped, x))]

    def group(wide, c0, base):
        return wide[:, c0 - base:c0 - base + LANE]

    pq = proj(C_Q, C_KS)
    for pr in range(NSA_HEADS // 2):
        slots = head_slots(group(pq, C_Q + pr * LANE, C_Q),
                           lambda i, mine, other: qc_ref[:, (2 * pr + i) * LANE:(2 * pr + i + 1) * LANE])
        for i, sl in enumerate(slots):
            q_o[:, (2 * pr + i) * LANE:(2 * pr + i + 1) * LANE] = sl.astype(BF16)
    pkv = proj(C_KS, C_KC)
    for c0, aug_ref, out in ((C_KS, kas_ref, ks_o), (C_KW, kaw_ref, kw_o)):
        slots = head_slots(group(pkv, c0, C_KS), lambda i, mine, other: aug_ref[:, i * LANE:(i + 1) * LANE])
        for i, sl in enumerate(slots):
            out[:, i * LANE:(i + 1) * LANE] = sl.astype(BF16)
    for c0, out in ((C_VS, vs_o), (C_VW, vw_o)):
        for i, sl in enumerate(head_slots(group(pkv, c0, C_KS), lambda i, mine, other: other)):
            out[:, i * LANE:(i + 1) * LANE] = sl.astype(BF16)
    pmisc = proj(C_KC, C_QB)
    for c0, out in ((C_KC, kc_o), (C_VC, vc_o)):
        chunk_ref[...] = group(pmisc, c0, C_KC)
        for tk in range(CMP_STRIDE):
            out[:, tk * LANE:(tk + 1) * LANE] = chunk_ref[pl.ds(tk, TM_WIDE // CMP_STRIDE, stride=CMP_STRIDE), :].astype(BF16)
    gt_o[...] = jax.nn.sigmoid(group(pmisc, C_GT, C_KC) + gb_ref[...]).astype(BF16)
    z = _dot(group(pmisc, C_AL, C_KC).astype(BF16), wa2_ref[...]) + ba_ref[...]
    log_sig = jnp.minimum(z, 0.0) - jnp.log1p(jnp.exp(-jnp.abs(z)))
    la_o[...] = log_sig * (1.0 / GLA_TAU)
    qb_o[...] = proj(C_QB, C_KB).astype(BF16)
    kb_o[...] = proj(C_KB, C_VB).astype(BF16)
    vb_o[...] = proj(C_VB, C_RB).astype(BF16)
    rb_o[...] = proj(C_RB, C_MG).astype(BF16)


def _inproj(x2, g, wcat, qconst, kaug_s, kaug_w, gate_b, wa2, ba):
    n = x2.shape[0]
    tiles_per_seq = SEQ // TM_WIDE
    outs = [(1, NSA_HEADS * LANE, BF16), (1, LANE * 2, BF16), (1, LANE * 2, BF16), (1, LANE * 2, BF16),
            (1, LANE * 2, BF16), (CMP_STRIDE, CMP_STRIDE * LANE, BF16), (CMP_STRIDE, CMP_STRIDE * LANE, BF16),
            (1, LANE, BF16), (1, GLA_KEY_WIDTH, BF16), (1, GLA_KEY_WIDTH, BF16),
            (1, GLA_VAL_WIDTH, BF16), (1, GLA_VAL_WIDTH, BF16), (1, GLA_KEY_WIDTH, F32), (1, 2 * D_MODEL, BF16)]
    tile = lambda w, per=1: pl.BlockSpec((TM_WIDE // per, w), lambda i: (i, 0))
    return pl.pallas_call(
        _inproj_body,
        grid=(n // TM_WIDE,),
        in_specs=[tile(D_MODEL), _const_spec((1, D_MODEL)), _const_spec((D_MODEL, C_END)),
                  _const_spec((1, NSA_HEADS * LANE)),
                  pl.BlockSpec((TM_WIDE, 2 * LANE), lambda i: (i % tiles_per_seq, 0)),
                  pl.BlockSpec((TM_WIDE, 2 * LANE), lambda i: (i % tiles_per_seq, 0)),
                  _const_spec((1, LANE)), _const_spec((LANE, GLA_KEY_WIDTH)), _const_spec((1, GLA_KEY_WIDTH))],
        out_specs=[tile(w, per) for per, w, _ in outs],
        out_shape=[jax.ShapeDtypeStruct((n // per, w), dt) for per, w, dt in outs],
        scratch_shapes=[pltpu.VMEM((TM_WIDE, LANE), F32)],
        compiler_params=pltpu.CompilerParams(dimension_semantics=("arbitrary",), vmem_limit_bytes=VMEM_LIMIT),
        name="inproj",
    )(x2, g, wcat, qconst, kaug_s, kaug_w, gate_b, wa2, ba)


def _memkv_body(m_ref, g_ref, w_ref, k_o, v_o):
    hm = _rms(m_ref[0], g_ref[...]).astype(BF16)
    kv = _dot(hm, w_ref[...])
    k_o[0] = kv[:, :XATTN_WIDTH].astype(BF16)
    v_o[0] = kv[:, XATTN_WIDTH:].astype(BF16)


def _memkv(mem, g, w):
    b = mem.shape[0]
    blk = pl.BlockSpec((1, MEM_LEN, XATTN_WIDTH), lambda i: (i, 0, 0))
    return pl.pallas_call(
        _memkv_body,
        grid=(b,),
        in_specs=[pl.BlockSpec((1, MEM_LEN, D_MODEL), lambda i: (i, 0, 0)), _const_spec((1, D_MODEL)),
                  _const_spec((D_MODEL, 2 * XATTN_WIDTH))],
        out_specs=[blk, blk],
        out_shape=[jax.ShapeDtypeStruct((b, MEM_LEN, XATTN_WIDTH), BF16)] * 2,
        compiler_params=pltpu.CompilerParams(dimension_semantics=("arbitrary",)),
        name="memkv",
    )(mem, g, w)


def _compress(c_ref, wa_ref, wb_ref, w2_ref, p1_ref, p2_ref):
    c = c_ref[0].astype(F32)
    a = _dot((c + p1_ref[...]).astype(BF16), wa_ref[0])
    b = _dot((c + p2_ref[...]).astype(BF16), wb_ref[0])
    hid = a + pltpu.roll(b, N_CMP_PAD - 1, axis=0)
    return _dot(jax.nn.gelu(hid).astype(BF16), w2_ref[...]).astype(BF16)


def _nsa_constants(slope_ref, biasc_ref, wbias_ref, dbias_ref):
    t_col = lax.broadcasted_iota(jnp.int32, (SEQ, LANE), 0)
    n_row = lax.broadcasted_iota(jnp.int32, (SEQ, LANE), 1)
    dist_c = (t_col - (n_row * CMP_STRIDE + (CMP_BLOCK - 1))).astype(F32)
    for g in range(NSA_GROUP):
        biasc_ref[g] = jnp.where(dist_c >= 0.0, -slope_ref[0, g:g + 1, :] * dist_c, NEG_INF)
    col = lax.broadcasted_iota(jnp.int32, (QUAD_ROWS, WIN_KEYS), 1)
    row = lax.broadcasted_iota(jnp.int32, (QUAD_ROWS, WIN_KEYS), 0)
    dist_w = ((row >> 8) << 6) + (row & (SLC_BLOCK - 1)) + WIN_PAD - col
    wbias_ref[...] = jnp.where((dist_w >= 0) & (dist_w < WINDOW), 0.0, NEG_INF)
    col = lax.broadcasted_iota(jnp.int32, (QUAD_ROWS, QUAD_TOKENS), 1)
    row = lax.broadcasted_iota(jnp.int32, (QUAD_ROWS, QUAD_TOKENS), 0)
    blk_r, blk_c = row >> 8, col >> 6
    causal = (col & (SLC_BLOCK - 1)) <= (row & (SLC_BLOCK - 1))
    dbias_ref[...] = jnp.where((blk_c < blk_r) | ((blk_c == blk_r) & causal), 0.0, NEG_INF)


def _nsa_body(q_ref, ks_ref, kw_ref, vs_ref, vw_ref, kc_ref, vc_ref, gt_ref,
              wak_ref, wbk_ref, w2k_ref, wav_ref, wbv_ref, w2v_ref, p1k_ref, p2k_ref, p1v_ref, p2v_ref,
              rrep_ref, slope_ref, ovl_ref, o_ref,
              ocmp_ref, qs_ref, biasc_ref, wbias_ref, dbias_ref, s_ref, e_ref, sw_ref, ew_ref):
    @pl.when(pl.program_id(1) == 0)
    def _():
        _nsa_constants(slope_ref, biasc_ref, wbias_ref, dbias_ref)

    kcmp = _compress(kc_ref, wak_ref, wbk_ref, w2k_ref, p1k_ref, p2k_ref)
    vcmp = _compress(vc_ref, wav_ref, wbv_ref, w2v_ref, p1v_ref, p2v_ref)

    low_half = lax.broadcasted_iota(jnp.int32, (SEQ, LANE), 1) < HEAD_DIM
    has_key = lax.broadcasted_iota(jnp.int32, (SEQ, 1), 0) >= CMP_BLOCK - 1
    imp = jnp.zeros((SEQ, LANE), F32)
    ones_sq = jnp.ones((LANE, LANE), BF16)
    o_even = None
    for g in range(NSA_GROUP):
        s = _dot_nt(q_ref[0, :, g * LANE:(g + 1) * LANE], kcmp) + biasc_ref[g]
        e = jnp.exp2(s - jnp.max(s, axis=-1, keepdims=True))
        l = _dot(e.astype(BF16), ones_sq)
        p = e * jnp.where(has_key, 1.0 / l, 0.0)
        imp = imp + p
        oc = _dot(p.astype(BF16), vcmp)
        if g % 2 == 0:
            o_even = oc
        else:
            ocmp_ref[g // 2] = jnp.where(low_half, o_even, oc)

    hi, lo = _split_bf16(imp.T)
    ovl = ovl_ref[...]
    imp_b = _dot(ovl, hi) + _dot(ovl, lo)
    j = lax.broadcasted_iota(jnp.int32, (N_SLC, SEQ), 0)
    cur = lax.broadcasted_iota(jnp.int32, (N_SLC, SEQ), 1) >> 6
    forced = (j == 0) | (j == cur) | (j == cur - 1)
    val = jnp.where(forced, float(NSA_GROUP + 1), jnp.where(j > cur, -1.0, imp_b))
    j8 = lax.broadcasted_iota(jnp.int32, (8, LANE), 0)
    rank_tiles = []
    for lt in range(SEQ // LANE):
        n_blk = (lt + 1) * LANE // SLC_BLOCK
        n_grp = -(-n_blk // 8)
        vt = val[:, lt * LANE:(lt + 1) * LANE]
        vals = [vt[8 * r:8 * r + 8, :] for r in range(n_grp)]
        ranks = [jnp.zeros((8, LANE), F32) for _ in vals]
        for jp in range(n_blk):
            vj = vt[jp:jp + 1, :]
            for r, vr in enumerate(vals):
                if 8 * r + 7 <= jp:
                    beats = vj > vr
                elif 8 * r > jp:
                    beats = vj >= vr
                else:
                    beats = (vj > vr) | ((vj == vr) & (j8 > jp - 8 * r))
                ranks[r] = ranks[r] + jnp.where(beats, 1.0, 0.0)
        ranks += [jnp.zeros((8, LANE), F32)] * (N_SLC // 8 - n_grp)
        rank_tiles.append(jnp.concatenate(ranks, axis=0))
    rank = jnp.concatenate(rank_tiles, axis=1)
    sel_bias = jnp.where((rank < float(N_SELECT)) & (j <= cur), 0.0, NEG_INF)
    aug_t = jnp.concatenate([jnp.zeros((AUG_SEL, SEQ), F32), sel_bias,
                             jnp.zeros((LANE - AUG_SEL - N_SLC, SEQ), F32)], axis=0)
    aug = aug_t.T.astype(BF16)
    for g in range(NSA_GROUP):
        qs_ref[g] = q_ref[0, :, g * LANE:(g + 1) * LANE] + aug

    lane64 = lax.broadcasted_iota(jnp.int32, (SLC_BLOCK, LANE), 1) < HEAD_DIM

    def softmax_pv(s_buf, e_buf, rows, nkeys, v):
        m = jnp.max(s_buf[rows, 0:nkeys], axis=-1, keepdims=True)
        e = jnp.exp2(s_buf[rows, 0:nkeys] - m)
        e_buf[rows, 0:nkeys] = e.astype(BF16)
        return _dot(e_buf[rows, 0:nkeys], v) * (1.0 / jnp.sum(e, axis=-1, keepdims=True))

    def quad(iq):
        r0 = iq * QUAD_TOKENS
        nkeys = r0 + QUAD_TOKENS
        qs = jnp.concatenate([qs_ref[g, r0 + bq * SLC_BLOCK:r0 + (bq + 1) * SLC_BLOCK, :]
                              for bq in range(QUAD_TOKENS // SLC_BLOCK) for g in range(NSA_GROUP)], axis=0)
        halves = [slice(hf * QUAD_ROWS // 2, (hf + 1) * QUAD_ROWS // 2) for hf in range(2)]
        for rows in halves:
            for c0 in range(0, r0, KEY_CHUNK):
                c1 = min(c0 + KEY_CHUNK, r0)
                s_ref[rows, c0:c1] = _dot_nt(qs[rows], ks_ref[0, c0:c1, :])
            s_ref[rows, r0:nkeys] = _dot_nt(qs[rows], ks_ref[0, r0:nkeys, :]) + dbias_ref[rows, :]
        w0 = max(r0 - WIN_PAD, 0)
        nwin = r0 + QUAD_TOKENS - w0
        sw_ref[:, 0:nwin] = _dot_nt(qs, kw_ref[0, w0:w0 + nwin, :]) + wbias_ref[:, WIN_KEYS - nwin:]
        o_s = jnp.concatenate([softmax_pv(s_ref, e_ref, rows, nkeys, vs_ref[0, 0:nkeys, :]) for rows in halves],
                              axis=0)
        o_w = softmax_pv(sw_ref, ew_ref, slice(None), nwin, vw_ref[0, w0:w0 + nwin, :])
        gt = _dot(gt_ref[0, r0:r0 + QUAD_TOKENS, :], rrep_ref[0])
        for bq in range(QUAD_TOKENS // SLC_BLOCK):
            tok = slice(r0 + bq * SLC_BLOCK, r0 + (bq + 1) * SLC_BLOCK)
            gtb = gt[bq * SLC_BLOCK:(bq + 1) * SLC_BLOCK]
            for pp in range(2):
                a0 = (bq * NSA_GROUP + 2 * pp) * SLC_BLOCK
                a1, a2 = a0 + SLC_BLOCK, a0 + 2 * SLC_BLOCK
                osl = jnp.where(lane64, o_s[a0:a1], o_s[a1:a2])
                owi = jnp.where(lane64, o_w[a0:a1], o_w[a1:a2])
                out = (gtb[:, pp * LANE:(pp + 1) * LANE] * ocmp_ref[pp, tok, :]
                       + gtb[:, (2 + pp) * LANE:(3 + pp) * LANE] * osl
                       + gtb[:, (4 + pp) * LANE:(5 + pp) * LANE] * owi)
                o_ref[0, tok, pp * LANE:(pp + 1) * LANE] = out.astype(BF16)

    one = jnp.minimum(pl.program_id(1) + 1, 1)
    for iq in range(SEQ // QUAD_TOKENS):
        def body(_, carry, iq=iq):
            quad(iq)
            return carry

        lax.fori_loop(0, one, body, 0)


def _nsa(q, ks, kw, vs, vw, kc2, vc2, gt, wak, wbk, w2k, wav, wbv, w2v, p1k, p2k, p1v, p2v, rrep, slopes, ovl):
    b = q.shape[0]
    hs = lambda w: pl.BlockSpec((1, SEQ, w), lambda h, i: (i, 0, h))
    perh = lambda s: pl.BlockSpec((1,) + s, lambda h, i: (h,) + (0,) * len(s))
    cblk = pl.BlockSpec((1, N_CMP_PAD, CMP_STRIDE * 2 * HEAD_DIM), lambda h, i: (i, 0, 0))
    w1 = (CMP_STRIDE * 2 * HEAD_DIM, CMP_HIDDEN)
    return pl.pallas_call(
        _nsa_body,
        grid=(NSA_KV_HEADS, b),
        in_specs=[hs(NSA_GROUP * LANE), hs(LANE), hs(LANE), hs(LANE), hs(LANE), cblk, cblk,
                  pl.BlockSpec((1, SEQ, LANE), lambda h, i: (i, 0, 0)),
                  perh(w1), perh(w1), _const_spec((CMP_HIDDEN, LANE)),
                  perh(w1), perh(w1), _const_spec((CMP_HIDDEN, LANE)),
                  _const_spec((1, w1[0])), _const_spec((1, w1[0])), _const_spec((1, w1[0])), _const_spec((1, w1[0])),
                  perh((LANE, 6 * LANE)), perh((8, LANE)), _const_spec((N_SLC, LANE))],
        out_specs=pl.BlockSpec((1, SEQ, 2 * LANE), lambda h, i: (i, 0, h)),
        out_shape=jax.ShapeDtypeStruct((b, SEQ, NSA_HEADS * HEAD_DIM), BF16),
        scratch_shapes=[pltpu.VMEM((2, SEQ, LANE), F32), pltpu.VMEM((NSA_GROUP, SEQ, LANE), BF16),
                        pltpu.VMEM((NSA_GROUP, SEQ, LANE), F32), pltpu.VMEM((QUAD_ROWS, WIN_KEYS), F32),
                        pltpu.VMEM((QUAD_ROWS, QUAD_TOKENS), F32),
                        pltpu.VMEM((QUAD_ROWS, SEQ), F32), pltpu.VMEM((QUAD_ROWS, SEQ), BF16),
                        pltpu.VMEM((QUAD_ROWS, WIN_KEYS), F32), pltpu.VMEM((QUAD_ROWS, WIN_KEYS), BF16)],
        compiler_params=pltpu.CompilerParams(dimension_semantics=("arbitrary", "arbitrary"),
                                             vmem_limit_bytes=VMEM_LIMIT),
        name="nsa",
    )(q, ks, kw, vs, vw, kc2, vc2, gt, wak, wbk, w2k, wav, wbv, w2v, p1k, p2k, p1v, p2v, rrep, slopes, ovl)


def _gla_body(q_ref, k_ref, v_ref, r_ref, la_ref, ng_ref, o_ref, oi_ref, qd_ref, inc_ref, dec_ref, st_ref):
    n_chunks = SEQ // GLA_CHUNK
    per_group = GLA_GROUP // GLA_CHUNK
    ri = lax.broadcasted_iota(jnp.int32, (GLA_GROUP, GLA_GROUP), 0)
    ci = lax.broadcasted_iota(jnp.int32, (GLA_GROUP, GLA_GROUP), 1)
    same_chunk = (ri >> 6) == (ci >> 6)
    tri = same_chunk & (ci <= ri)
    t_mat = jnp.where(tri, 1.0, 0.0).astype(BF16)
    o_mat = jnp.where(same_chunk, 1.0, 0.0).astype(BF16)
    head_g = lax.broadcasted_iota(jnp.int32, (GLA_GROUP, GLA_KEY_WIDTH), 1) >> 6
    head_c = lax.broadcasted_iota(jnp.int32, (GLA_CHUNK, GLA_KEY_WIDTH), 1) >> 6
    head_s = lax.broadcasted_iota(jnp.int32, (GLA_DV, GLA_KEY_WIDTH), 1) >> 6

    def group(gi):
        rows = slice(gi * GLA_GROUP, (gi + 1) * GLA_GROUP)
        hi, lo = _split_bf16(la_ref[0, rows, :])
        b = _dot(t_mat, hi) + _dot(t_mat, lo)
        b_last = _dot(o_mat, hi) + _dot(o_mat, lo)
        k = k_ref[0, rows, :].astype(F32)
        q_d = q_ref[0, rows, :].astype(F32) * jnp.exp(b)
        k_d = (k * jnp.exp(-b)).astype(BF16)
        k_s = (k * jnp.exp(b_last - b)).astype(BF16)
        qd_ref[rows, :] = q_d.astype(BF16)
        v = v_ref[0, rows, :]
        for h in range(GLA_HEADS):
            qm = jnp.where(head_g == h, q_d, 0.0).astype(BF16)
            att = jnp.where(tri, _dot_nt(qm, k_d), 0.0)
            oi_ref[rows, h * GLA_DV:(h + 1) * GLA_DV] = _dot(att.astype(BF16), v[:, h * GLA_DV:(h + 1) * GLA_DV])
        decay = jnp.exp(b_last)
        for c in range(per_group):
            cr = slice(c * GLA_CHUNK, (c + 1) * GLA_CHUNK)
            inc_full = _dot_tn(v[cr, :], k_s[cr, :])
            inc = jnp.zeros((GLA_DV, GLA_KEY_WIDTH), F32)
            for h in range(GLA_HEADS):
                inc = inc + jnp.where(head_s == h, inc_full[h * GLA_DV:(h + 1) * GLA_DV, :], 0.0)
            inc_ref[gi * per_group + c] = inc
            dec_ref[gi * per_group + c] = decay[c * GLA_CHUNK:c * GLA_CHUNK + 8, :]

    def emit(gi):
        for c in range(per_group):
            rows = slice(gi * GLA_GROUP + c * GLA_CHUNK, gi * GLA_GROUP + (c + 1) * GLA_CHUNK)
            q_d = qd_ref[rows, :]
            st = st_ref[gi * per_group + c]
            for h in range(GLA_HEADS):
                cols = slice(h * GLA_DV, (h + 1) * GLA_DV)
                qm = jnp.where(head_c == h, q_d, jnp.zeros_like(q_d))
                o = oi_ref[rows, cols] + _dot_nt(qm, st)
                o = o * lax.rsqrt(jnp.mean(o * o, axis=-1, keepdims=True) + RMS_EPS) * ng_ref[...]
                r = r_ref[0, rows, cols].astype(F32)
                o_ref[0, rows, cols] = (o * (r * jax.nn.sigmoid(r))).astype(BF16)

    state = jnp.zeros((GLA_DV, GLA_KEY_WIDTH), F32)
    for gi in range(SEQ // GLA_GROUP):
        group(gi)
        for n in range(gi * per_group, (gi + 1) * per_group):
            st_ref[n] = state.astype(BF16)
            state = state * dec_ref[n, 0:1, :] + inc_ref[n]
        emit(gi)


def _gla(qb, kb, vb, rb, la, ng):
    b = qb.shape[0]
    n_chunks = SEQ // GLA_CHUNK
    blk = lambda w: pl.BlockSpec((1, SEQ, w), lambda i: (i, 0, 0))
    return pl.pallas_call(
        _gla_body,
        grid=(b,),
        in_specs=[blk(GLA_KEY_WIDTH), blk(GLA_KEY_WIDTH), blk(GLA_VAL_WIDTH), blk(GLA_VAL_WIDTH),
                  blk(GLA_KEY_WIDTH), _const_spec((1, GLA_DV))],
        out_specs=blk(GLA_VAL_WIDTH),
        out_shape=jax.ShapeDtypeStruct((b, SEQ, GLA_VAL_WIDTH), BF16),
        scratch_shapes=[pltpu.VMEM((SEQ, GLA_VAL_WIDTH), F32), pltpu.VMEM((SEQ, GLA_KEY_WIDTH), BF16),
                        pltpu.VMEM((n_chunks, GLA_DV, GLA_KEY_WIDTH), F32),
                        pltpu.VMEM((n_chunks, 8, GLA_KEY_WIDTH), F32),
                        pltpu.VMEM((n_chunks, GLA_DV, GLA_KEY_WIDTH), BF16)],
        compiler_params=pltpu.CompilerParams(dimension_semantics=("arbitrary",), vmem_limit_bytes=VMEM_LIMIT),
        name="gla",
    )(qb, kb, vb, rb, la, ng)


def _mixx_body(x_ref, on_ref, og_ref, gm_ref, wn_ref, wg_ref, wo_ref, lx_ref, wq_ref, mem_ref, lm_ref, wkv_ref,
               wxo_ref, o_ref, kv_ref):
    @pl.when(pl.program_id(0) % (SEQ // TM_WIDE) == 0)
    def _():
        kv_ref[...] = _dot(_rms(mem_ref[0], lm_ref[...]).astype(BF16), wkv_ref[...]).astype(BF16)

    gm = gm_ref[...].astype(F32)
    mix = (gm[:, :D_MODEL] * _dot(on_ref[...], wn_ref[...])
           + gm[:, D_MODEL:] * _dot(og_ref[...], wg_ref[...]))
    x1 = x_ref[...] + _dot(mix.astype(BF16), wo_ref[...])
    hx = _rms(x1, lx_ref[...]).astype(BF16)
    q = (_dot(hx, wq_ref[...]) * (XATTN_HEAD_DIM ** -0.5)).astype(BF16)
    outs = []
    for h in range(XATTN_HEADS):
        sl = slice(h * XATTN_HEAD_DIM, (h + 1) * XATTN_HEAD_DIM)
        s = _dot_nt(q[:, sl], kv_ref[:, sl])
        e = jnp.exp(s - jnp.max(s, axis=-1, keepdims=True))
        l = jnp.sum(e, axis=-1, keepdims=True)
        vsl = slice(XATTN_WIDTH + h * XATTN_HEAD_DIM, XATTN_WIDTH + (h + 1) * XATTN_HEAD_DIM)
        outs.append((_dot(e.astype(BF16), kv_ref[:, vsl]) / l).astype(BF16))
    o = jnp.concatenate(outs, axis=-1)
    o_ref[...] = x1 + _dot(o, wxo_ref[...])


def _mixx(x2, o_nsa, o_gla, gm, wn, wg, wo, lx, wq, mem, lm, wkv, wxo):
    n = x2.shape[0]
    tiles_per_seq = SEQ // TM_WIDE
    tile = lambda w: pl.BlockSpec((TM_WIDE, w), lambda i: (i, 0))
    mblk = pl.BlockSpec((1, MEM_LEN, D_MODEL), lambda i: (i // tiles_per_seq, 0, 0))
    return pl.pallas_call(
        _mixx_body,
        grid=(n // TM_WIDE,),
        in_specs=[tile(D_MODEL), tile(XATTN_WIDTH), tile(GLA_VAL_WIDTH), tile(2 * D_MODEL),
                  _const_spec((XATTN_WIDTH, D_MODEL)), _const_spec((GLA_VAL_WIDTH, D_MODEL)),
                  _const_spec((D_MODEL, D_MODEL)), _const_spec((1, D_MODEL)),
                  _const_spec((D_MODEL, XATTN_WIDTH)), mblk, _const_spec((1, D_MODEL)),
                  _const_spec((D_MODEL, 2 * XATTN_WIDTH)), _const_spec((XATTN_WIDTH, D_MODEL))],
        out_specs=tile(D_MODEL),
        out_shape=jax.ShapeDtypeStruct((n, D_MODEL), F32),
        scratch_shapes=[pltpu.VMEM((MEM_LEN, 2 * XATTN_WIDTH), BF16)],
        compiler_params=pltpu.CompilerParams(dimension_semantics=("arbitrary",), vmem_limit_bytes=VMEM_LIMIT),
        name="mixx",
    )(x2, o_nsa, o_gla, gm, wn, wg, wo, lx, wq, mem, lm, wkv, wxo)


def _ffn_body(x_ref, lf_ref, wu_ref, wgt_ref, cw_ref, cb_ref, wd_ref, lfin_ref, o_ref, ubuf_ref, carry_ref):
    @pl.when(pl.program_id(0) % (SEQ // TM) == 0)
    def _():
        carry_ref[...] = jnp.zeros_like(carry_ref)

    x = x_ref[...]
    hf = _rms(x, lf_ref[...]).astype(BF16)
    acc = jnp.zeros((TM, D_MODEL), F32)
    for c0 in range(0, FFN_DIM, FFN_CHUNK):
        cols = slice(c0, min(c0 + FFN_CHUNK, FFN_DIM))
        w = cols.stop - cols.start
        ubuf_ref[0:8, 0:w] = carry_ref[:, cols]
        ubuf_ref[8:, 0:w] = _dot(hf, wu_ref[:, cols])
        carry_ref[:, cols] = ubuf_ref[TM:TM + 8, 0:w]
        u = (cw_ref[0:1, cols] * ubuf_ref[6:6 + TM, 0:w] + cw_ref[1:2, cols] * ubuf_ref[7:7 + TM, 0:w]
             + cw_ref[2:3, cols] * ubuf_ref[8:8 + TM, 0:w] + cb_ref[:, cols])
        act = (jax.nn.gelu(u) * _dot(hf, wgt_ref[:, cols])).astype(BF16)
        acc = acc + _dot(act, wd_ref[cols, :])
    o_ref[...] = _rms(x + acc, lfin_ref[...])


def _ffn(x2, lf, wu, wgt, cw, cb, wd, lfin):
    n = x2.shape[0]
    tile = pl.BlockSpec((TM, D_MODEL), lambda i: (i, 0))
    return pl.pallas_call(
        _ffn_body,
        grid=(n // TM,),
        in_specs=[tile, _const_spec((1, D_MODEL)), _const_spec((D_MODEL, FFN_DIM)), _const_spec((D_MODEL, FFN_DIM)),
                  _const_spec((CONV_WIDTH, FFN_DIM)), _const_spec((1, FFN_DIM)), _const_spec((FFN_DIM, D_MODEL)),
                  _const_spec((1, D_MODEL))],
        out_specs=tile,
        out_shape=jax.ShapeDtypeStruct((n, D_MODEL), F32),
        scratch_shapes=[pltpu.VMEM((TM + 8, FFN_CHUNK), F32), pltpu.VMEM((8, FFN_DIM), F32)],
        compiler_params=pltpu.CompilerParams(dimension_semantics=("arbitrary",), vmem_limit_bytes=VMEM_LIMIT),
        name="ffn",
    )(x2, lf, wu, wgt, cw, cb, wd, lfin)


def _layout_inproj(w_in):
    o = 0
    seg = {}
    for name, wdt in (("q", 512), ("kc", 128), ("vc", 128), ("ks", 128), ("vs", 128), ("kw", 128), ("vw", 128),
                      ("gt", 24), ("qb", 256), ("kb", 256), ("vb", 512), ("rb", 512), ("al", 16), ("mg", 2048)):
        seg[name] = w_in[:, o:o + wdt]
        o += wdt
    q_scale = (HEAD_DIM ** -0.5) * LOG2E
    cols = [seg["q"] * q_scale, seg["ks"], seg["kw"], seg["vs"], seg["vw"], seg["kc"], seg["vc"]]
    cols += [jnp.pad(seg["gt"], ((0, 0), (0, LANE - 3 * NSA_HEADS)))]
    cols += [jnp.pad(seg["al"], ((0, 0), (0, LANE - GLA_RANK)))]
    cols += [seg["qb"] * (GLA_DK ** -0.5), seg["kb"], seg["vb"], seg["rb"]]
    cols += [seg["mg"]]
    return jnp.concatenate(cols, axis=1).astype(BF16)


def _position_constants():
    slopes = 2.0 ** (-np.arange(1, NSA_HEADS + 1, dtype=np.float64)) * LOG2E
    qconst = np.zeros((1, NSA_HEADS * LANE), np.float32)
    for hh in range(NSA_HEADS):
        qconst[0, hh * LANE + AUG_HI] = slopes[hh]
        qconst[0, hh * LANE + AUG_LO] = slopes[hh]
    t = np.arange(SEQ)
    kaug_s = np.zeros((SEQ, 2 * LANE), np.float32)
    kaug_w = np.zeros((SEQ, 2 * LANE), np.float32)
    for h in range(NSA_KV_HEADS):
        kaug_s[t, h * LANE + AUG_SEL + t // SLC_BLOCK] = 1.0
        for a in (kaug_s, kaug_w):
            a[:, h * LANE + AUG_HI] = (t // SLC_BLOCK) * SLC_BLOCK
            a[:, h * LANE + AUG_LO] = t % SLC_BLOCK
    slope_rows = np.zeros((NSA_KV_HEADS, 8, LANE), np.float32)
    rrep = np.zeros((NSA_KV_HEADS, LANE, 6 * LANE), np.float32)
    for h in range(NSA_KV_HEADS):
        for g in range(NSA_GROUP):
            slope_rows[h, g, :] = slopes[h * NSA_GROUP + g]
            for c in range(3):
                pp, half = g // 2, g % 2
                c0 = (c * 2 + pp) * LANE + half * HEAD_DIM
                rrep[h, (h * NSA_GROUP + g) * 3 + c, c0:c0 + HEAD_DIM] = 1.0
    ovl = np.zeros((N_SLC, N_CMP_PAD), np.float32)
    for n in range(N_CMP_PAD - 1):
        for tok in range(n * CMP_STRIDE, n * CMP_STRIDE + CMP_BLOCK):
            ovl[tok // SLC_BLOCK, n] += 1.0 / CMP_BLOCK
    return qconst, kaug_s, kaug_w, slope_rows, rrep, ovl


def _layout_compress(w1, w2, pos, dup):
    w1 = w1.reshape(2, CMP_STRIDE, 1, HEAD_DIM, CMP_HIDDEN)
    sel = jnp.eye(NSA_KV_HEADS, dtype=F32)[:, None, None, :, None, None]
    wh = (w1[None] * sel).reshape(NSA_KV_HEADS, 2, CMP_STRIDE * NSA_KV_HEADS * HEAD_DIM, CMP_HIDDEN)
    p = jnp.broadcast_to(pos.reshape(2, CMP_STRIDE, 1, HEAD_DIM), (2, CMP_STRIDE, NSA_KV_HEADS, HEAD_DIM))
    p = p.reshape(2, 1, CMP_STRIDE * NSA_KV_HEADS * HEAD_DIM)
    w2p = jnp.concatenate([w2, w2 if dup else jnp.zeros_like(w2)], axis=1)
    return wh[:, 0].astype(BF16), wh[:, 1].astype(BF16), w2p.astype(BF16), p[0], p[1]


def kernel(x, mem, ln_mix_g, w_in, nsa_gate_b, cmp_pos_k, cmp_w1_k, cmp_w2_k, cmp_pos_v, cmp_w1_v, cmp_w2_v,
           gla_w_alpha2, gla_b_alpha, gla_norm_g, w_branch_nsa, w_branch_gla, w_out, ln_x_g, ln_mem_g, w_xq,
           w_xkv, w_xo, ln_ffn_g, w_up, conv_w, conv_b, w_down, ln_final_g):
    b = x.shape[0]
    n = b * SEQ
    row = lambda v: v.reshape(1, -1).astype(F32)
    qconst, kaug_s, kaug_w, slope_rows, rrep, ovl = _position_constants()

    x2 = x.reshape(n, D_MODEL)
    gate_b = jnp.pad(nsa_gate_b[0], (0, LANE - 3 * NSA_HEADS)).reshape(1, LANE)
    wa2 = jnp.pad(gla_w_alpha2[0], ((0, LANE - GLA_RANK), (0, 0))).astype(BF16)
    (q, ks, kw, vs, vw, kc, vc, gt, qb, kb, vb, rb, la, gm) = _inproj(
        x2, row(ln_mix_g[0]), _layout_inproj(w_in[0]), qconst, kaug_s, kaug_w, gate_b, wa2, row(gla_b_alpha[0]))

    s3 = lambda a: a.reshape(b, SEQ, a.shape[-1])
    chunked = lambda a: a.reshape(b, N_CMP_PAD, a.shape[-1])
    wak, wbk, w2k, p1k, p2k = _layout_compress(cmp_w1_k[0], cmp_w2_k[0], cmp_pos_k[0], dup=False)
    wav, wbv, w2v, p1v, p2v = _layout_compress(cmp_w1_v[0], cmp_w2_v[0], cmp_pos_v[0], dup=True)
    o_nsa = _nsa(s3(q), s3(ks), s3(kw), s3(vs), s3(vw), chunked(kc), chunked(vc), s3(gt),
                 wak, wbk, w2k, wav, wbv, w2v, p1k, p2k, p1v, p2v,
                 jnp.asarray(rrep, BF16), jnp.asarray(slope_rows), jnp.asarray(ovl, BF16))

    o_gla = _gla(s3(qb), s3(kb), s3(vb), s3(rb), s3(la), row(gla_norm_g[0]))

    x_mid = _mixx(x2, o_nsa.reshape(n, -1), o_gla.reshape(n, -1), gm,
                  w_branch_nsa[0].astype(BF16), w_branch_gla[0].astype(BF16), w_out[0].astype(BF16),
                  row(ln_x_g[0]), w_xq[0].astype(BF16), mem, row(ln_mem_g[0]), w_xkv[0].astype(BF16),
                  w_xo[0].astype(BF16))

    out = _ffn(x_mid, row(ln_ffn_g[0]), w_up[0][:, :FFN_DIM].astype(BF16), w_up[0][:, FFN_DIM:].astype(BF16),
               conv_w[0], conv_b[0].reshape(1, FFN_DIM), w_down[0].astype(BF16), row(ln_final_g))
    return out.reshape(b, SEQ, D_MODEL)
```
